```python
import math
import jax, jax.numpy as jnp
from jax import lax
import numpy as np


D_MODEL = 4096
BATCH = 4
SEQ = 2048
DEPTH = 2
DEC_BATCH = 8
DEC_SEQ = 8
PAST_LEN = 16384
PAGE_SIZE = 128

N_MIXERS = 2
N_SSM_LAYERS = (DEPTH + 1) // 2
N_ATTN_LAYERS = DEPTH // 2

SSM_EXPAND = 2
D_INNER = SSM_EXPAND * D_MODEL
SSM_HEAD_DIM = 64
SSM_HEADS = D_INNER // SSM_HEAD_DIM
SSM_GROUPS = 8
SSM_STATE = 128
CONV_WIDTH = 4
GN = SSM_GROUPS * SSM_STATE
CONV_DIM = D_INNER + 2 * GN
SSM_IN_DIM = D_INNER + CONV_DIM + SSM_HEADS
SSM_CHUNK = 128
DT_MIN = 0.001
DT_MAX = 0.1

ATTN_HEADS = 16
ATTN_HEAD_DIM = D_MODEL // ATTN_HEADS // 2
ATTN_V_DIM = 2 * ATTN_HEAD_DIM
QK_DIM = 2 * ATTN_HEADS * ATTN_HEAD_DIM
ATTN_WIDTH = ATTN_HEADS * ATTN_V_DIM
ATTN_IN_DIM = 2 * QK_DIM + 2 * ATTN_WIDTH
Q_BLOCK = 128
ROPE_THETA = 10000.0

NORM_EPS = 1e-5
DEEPNORM_ALPHA = (2 * DEPTH) ** 0.25
DEEPNORM_BETA = (8 * DEPTH) ** -0.25

kernel_name = 'hybrid_ssd_diffattn_deepnorm_step'

F32 = jnp.float32


def layer_norm(x, g, b):
    xf = x.astype(F32)
    mu = jnp.mean(xf, -1, keepdims=True)
    var = jnp.mean(jnp.square(xf - mu), -1, keepdims=True)
    return ((xf - mu) * lax.rsqrt(var + NORM_EPS) * g.astype(F32) + b.astype(F32)).astype(x.dtype)


def rms_norm(x, g):
    xf = x.astype(F32)
    return xf * lax.rsqrt(jnp.mean(jnp.square(xf), -1, keepdims=True) + NORM_EPS) * g.astype(F32)


def causal_dwconv(x, buf, w, b):
    xp = jnp.concatenate([buf.astype(x.dtype), x], axis=1)
    y = lax.conv_general_dilated(xp, w.astype(x.dtype)[:, None, :], window_strides=(1,), padding='VALID',
                                 dimension_numbers=('NWC', 'WIO', 'NWC'), feature_group_count=x.shape[-1])
    return y + b.astype(x.dtype), xp[:, -(CONV_WIDTH - 1):]


def ssd_chunked(x, dt, a, bm, cm, h0):
    bsz, t = x.shape[:2]
    L = min(SSM_CHUNK, t)
    pad = (-t) % L
    x, dt, bm, cm = (v.astype(F32) for v in (x, dt, bm, cm))
    if pad:
        padw = lambda v: jnp.pad(v, [(0, 0), (0, pad)] + [(0, 0)] * (v.ndim - 2))
        x, dt, bm, cm = padw(x), padw(dt), padw(bm), padw(cm)
    nc = (t + pad) // L
    hg = SSM_HEADS // SSM_GROUPS
    xd = (x * dt[..., None]).reshape(bsz, nc, L, SSM_GROUPS, hg, SSM_HEAD_DIM)
    da = (dt * a).reshape(bsz, nc, L, SSM_GROUPS, hg)
    bc = bm.reshape(bsz, nc, L, SSM_GROUPS, SSM_STATE)
    cc = cm.reshape(bsz, nc, L, SSM_GROUPS, SSM_STATE)
    a_cs = jnp.cumsum(da, axis=2)
    causal = jnp.asarray(np.tril(np.ones((L, L), bool)))[:, :, None, None]
    seg = a_cs[:, :, :, None] - a_cs[:, :, None, :]
    decay = jnp.where(causal, jnp.exp(jnp.where(causal, seg, 0.0)), 0.0)
    cb = jnp.einsum('bclgn,bcsgn->bclsg', cc, bc)
    y_diag = jnp.einsum('bclsg,bclsgh,bcsghp->bclghp', cb, decay, xd)
    decay_end = jnp.exp(a_cs[:, :, -1:] - a_cs)
    chunk_states = jnp.einsum('bclgn,bclgh,bclghp->bcghpn', bc, decay_end, xd)
    chunk_decay = jnp.exp(a_cs[:, :, -1])

    def step(h, inp):
        s_c, d_c = inp
        return d_c[..., None, None] * h + s_c, h

    h_init = h0.astype(F32).reshape(bsz, SSM_GROUPS, hg, SSM_HEAD_DIM, SSM_STATE)
    h_last, h_prev = lax.scan(step, h_init, (chunk_states.swapaxes(0, 1), chunk_decay.swapaxes(0, 1)))
    h_prev = h_prev.swapaxes(0, 1)
    y_off = jnp.einsum('bclgn,bcghpn,bclgh->bclghp', cc, h_prev, jnp.exp(a_cs))
    y = (y_diag + y_off).reshape(bsz, nc * L, SSM_HEADS, SSM_HEAD_DIM)[:, :t]
    return y, h_last.reshape(bsz, SSM_HEADS, SSM_HEAD_DIM, SSM_STATE)


def mamba2_mixer(x, conv_buf, h0, w_in, conv_w, conv_b, dt_bias, a_log, d_skip, norm_w, w_out):
    bsz, t, _ = x.shape
    z, xbc, dt = jnp.split(x @ w_in, [D_INNER, D_INNER + CONV_DIM], axis=-1)
    xbc, new_buf = causal_dwconv(xbc, conv_buf, conv_w, conv_b)
    xbc = jax.nn.silu(xbc)
    xs, bm, cm = jnp.split(xbc, [D_INNER, D_INNER + GN], axis=-1)
    xs = xs.reshape(bsz, t, SSM_HEADS, SSM_HEAD_DIM).astype(F32)
    dt = jax.nn.softplus(dt.astype(F32) + dt_bias.astype(F32))
    a = -jnp.exp(a_log.astype(F32))
    y, h_new = ssd_chunked(xs, dt, a, bm.reshape(bsz, t, SSM_GROUPS, SSM_STATE),
                           cm.reshape(bsz, t, SSM_GROUPS, SSM_STATE), h0)
    y = y + d_skip.astype(F32)[:, None] * xs
    y = y.reshape(bsz, t, D_INNER) * jax.nn.silu(z.astype(F32))
    yg = y.reshape(bsz, t, SSM_GROUPS, D_INNER // SSM_GROUPS)
    yg = yg * lax.rsqrt(jnp.mean(jnp.square(yg), -1, keepdims=True) + NORM_EPS)
    y = (yg.reshape(bsz, t, D_INNER) * norm_w.astype(F32)).astype(x.dtype)
    return y @ w_out, new_buf, h_new.astype(h0.dtype)


def rope(x, pos):
    half = x.shape[-1] // 2
    inv = ROPE_THETA ** (-jnp.arange(half, dtype=F32) / half)
    ang = pos.astype(F32)[:, None] * inv[None]
    cos, sin = jnp.cos(ang)[None, :, None], jnp.sin(ang)[None, :, None]
    xf = x.astype(F32)
    x1, x2 = xf[..., :half], xf[..., half:]
    return jnp.concatenate([x1 * cos - x2 * sin, x2 * cos + x1 * sin], -1).astype(x.dtype)


def diff_attn_project(x, pos, w_in):
    bsz, t, _ = x.shape
    q, k, v, g = jnp.split(x @ w_in, [QK_DIM, 2 * QK_DIM, 2 * QK_DIM + ATTN_WIDTH], axis=-1)
    q = rope(q.reshape(bsz, t, 2 * ATTN_HEADS, ATTN_HEAD_DIM), pos)
    k = rope(k.reshape(bsz, t, 2 * ATTN_HEADS, ATTN_HEAD_DIM), pos)
    v = v.reshape(bsz, t, ATTN_HEADS, ATTN_V_DIM)
    return q, k, v, g


def block_stats(q, k, v, mask):
    s = jnp.einsum('bqhcd,bkhcd->bhcqk', q, k, preferred_element_type=F32)
    if mask is not None:
        s = jnp.where(mask, s, -jnp.inf)
    m = jnp.max(s, -1)
    p = jnp.exp(s - m[..., None])
    return m, jnp.sum(p, -1), jnp.einsum('bhcqk,bkhe->bhcqe', p, v.astype(F32))


def merge_stats(sa, sb):
    m = jnp.maximum(sa[0], sb[0])
    ca, cb = jnp.exp(sa[0] - m), jnp.exp(sb[0] - m)
    return m, sa[1] * ca + sb[1] * cb, sa[2] * ca[..., None] + sb[2] * cb[..., None]


def diff_combine(stats, lam):
    o = stats[2] / stats[1][..., None]
    return (o[:, :, 0] - lam * o[:, :, 1]).transpose(0, 2, 1, 3)


def diff_attn_prompt(q, k, v, lam):
    bsz, t = q.shape[:2]
    qb = min(Q_BLOCK, t)
    nb = t // qb
    qs = (q * ATTN_HEAD_DIM ** -0.5).reshape(bsz, nb, qb, ATTN_HEADS, 2, ATTN_HEAD_DIM).swapaxes(0, 1)
    kh = k.reshape(bsz, t, ATTN_HEADS, 2, ATTN_HEAD_DIM)
    kpos = jnp.arange(t)

    def one_block(args):
        q_blk, start = args
        mask = kpos[None, :] <= (start + jnp.arange(qb))[:, None]
        return diff_combine(block_stats(q_blk, kh, v, mask), lam)

    out = lax.map(one_block, (qs, jnp.arange(nb) * qb))
    return out.swapaxes(0, 1).reshape(bsz, t, ATTN_HEADS, ATTN_V_DIM)


def diff_attn_sample(q, k, v, cache_k, cache_v, layer, page_table, lam):
    bsz, t = q.shape[:2]
    qs = (q * ATTN_HEAD_DIM ** -0.5).reshape(bsz, t, ATTN_HEADS, 2, ATTN_HEAD_DIM)

    def page_step(carry, pids):
        kp = cache_k[layer, pids].reshape(bsz, PAGE_SIZE, ATTN_HEADS, 2, ATTN_HEAD_DIM)
        vp = cache_v[layer, pids]
        return merge_stats(carry, block_stats(qs, kp, vp, None)), None

    init = (jnp.full((bsz, ATTN_HEADS, 2, t), -jnp.inf, F32), jnp.zeros((bsz, ATTN_HEADS, 2, t), F32),
            jnp.zeros((bsz, ATTN_HEADS, 2, t, ATTN_V_DIM), F32))
    stats, _ = lax.scan(page_step, init, page_table.T)
    mask = jnp.asarray(np.tril(np.ones((t, t), bool)))
    stats = merge_stats(stats, block_stats(qs, k.reshape(bsz, t, ATTN_HEADS, 2, ATTN_HEAD_DIM), v, mask))
    return diff_combine(stats, lam)


def diff_attn_output(o, g, lam_init, subln_w, w_out):
    bsz, t = o.shape[:2]
    o = rms_norm(o, subln_w) * (1.0 - lam_init)
    o = o.reshape(bsz, t, ATTN_WIDTH) * jax.nn.silu(g.astype(F32))
    return o.astype(g.dtype) @ w_out


def setup_inputs(seed: int = 0) -> dict:
    key = jax.random.key(seed)
    ks = iter(jax.random.split(key, 32))
    nrm = lambda shape, scale: scale * jax.random.normal(next(ks), shape, F32)
    n_pages = PAST_LEN // PAGE_SIZE
    n_pool = (5 * DEC_BATCH * n_pages) // 4
    page_table = jax.random.permutation(next(ks), n_pool)[:DEC_BATCH * n_pages]
    page_table = page_table.reshape(DEC_BATCH, n_pages).astype(jnp.int32)
    dt0 = jnp.exp(jax.random.uniform(next(ks), (N_SSM_LAYERS, SSM_HEADS), F32)
                  * (math.log(DT_MAX) - math.log(DT_MIN)) + math.log(DT_MIN))
    return {
        'x_prompt': nrm((BATCH, SEQ, D_MODEL), 1.0),
        'x_sample': nrm((DEC_BATCH, DEC_SEQ, D_MODEL), 1.0),
        'state_conv': nrm((N_SSM_LAYERS, DEC_BATCH, CONV_WIDTH - 1, CONV_DIM), 1.0),
        'state_ssm': nrm((N_SSM_LAYERS, DEC_BATCH, SSM_HEADS, SSM_HEAD_DIM, SSM_STATE), 0.1),
        'cache_k': nrm((N_ATTN_LAYERS, n_pool, PAGE_SIZE, 2 * ATTN_HEADS, ATTN_HEAD_DIM), 1.0),
        'cache_v': nrm((N_ATTN_LAYERS, n_pool, PAGE_SIZE, ATTN_HEADS, ATTN_V_DIM), 1.0),
        'page_table': page_table,
        'ssm_w_in': nrm((N_SSM_LAYERS, D_MODEL, SSM_IN_DIM), D_MODEL ** -0.5),
        'ssm_conv_w': nrm((N_SSM_LAYERS, CONV_WIDTH, CONV_DIM), CONV_WIDTH ** -0.5),
        'ssm_conv_b': nrm((N_SSM_LAYERS, CONV_DIM), 0.02),
        'ssm_dt_bias': dt0 + jnp.log(-jnp.expm1(-dt0)),
        'ssm_a_log': jnp.log(jax.random.uniform(next(ks), (N_SSM_LAYERS, SSM_HEADS), F32, 1.0, 16.0)),
        'ssm_d': 1.0 + nrm((N_SSM_LAYERS, SSM_HEADS), 0.1),
        'ssm_norm_w': 1.0 + nrm((N_SSM_LAYERS, D_INNER), 0.1),
        'ssm_w_out': nrm((N_SSM_LAYERS, D_INNER, D_MODEL), DEEPNORM_BETA * D_INNER ** -0.5),
        'attn_w_in': nrm((N_ATTN_LAYERS, D_MODEL, ATTN_IN_DIM), D_MODEL ** -0.5),
        'attn_lambda_q1': nrm((N_ATTN_LAYERS, ATTN_HEAD_DIM), 0.1),
        'attn_lambda_k1': nrm((N_ATTN_LAYERS, ATTN_HEAD_DIM), 0.1),
        'attn_lambda_q2': nrm((N_ATTN_LAYERS, ATTN_HEAD_DIM), 0.1),
        'attn_lambda_k2': nrm((N_ATTN_LAYERS, ATTN_HEAD_DIM), 0.1),
        'attn_subln_w': 1.0 + nrm((N_ATTN_LAYERS, ATTN_V_DIM), 0.1),
        'attn_w_out': nrm((N_ATTN_LAYERS, ATTN_WIDTH, D_MODEL), DEEPNORM_BETA * ATTN_WIDTH ** -0.5),
        'ln_g': 1.0 + nrm((DEPTH, D_MODEL), 0.1),
        'ln_b': nrm((DEPTH, D_MODEL), 0.02),
    }


def reference(x_prompt, x_sample, state_conv, state_ssm, cache_k, cache_v, page_table,
              ssm_w_in, ssm_conv_w, ssm_conv_b, ssm_dt_bias, ssm_a_log, ssm_d, ssm_norm_w, ssm_w_out,
              attn_w_in, attn_lambda_q1, attn_lambda_k1, attn_lambda_q2, attn_lambda_k2,
              attn_subln_w, attn_w_out, ln_g, ln_b):
    bp, tp = x_prompt.shape[:2]
    bs, ts = x_sample.shape[:2]
    pos_p = jnp.arange(tp)
    pos_s = PAST_LEN + jnp.arange(ts)
    yp, ys = x_prompt, x_sample
    conv_p, ssm_p, k_p, v_p = [], [], [], []
    conv_s, ssm_s, k_s, v_s = [], [], [], []
    for i in range(DEPTH):
        j = i // N_MIXERS
        if i % N_MIXERS == 0:
            ssm_w = (ssm_w_in[j], ssm_conv_w[j], ssm_conv_b[j], ssm_dt_bias[j], ssm_a_log[j],
                     ssm_d[j], ssm_norm_w[j], ssm_w_out[j])
            buf0 = jnp.zeros((bp, CONV_WIDTH - 1, CONV_DIM), yp.dtype)
            h0 = jnp.zeros((bp, SSM_HEADS, SSM_HEAD_DIM, SSM_STATE), F32)
            mp, cb_p, h_p = mamba2_mixer(yp, buf0, h0, *ssm_w)
            ms, cb_s, h_s = mamba2_mixer(ys, state_conv[j], state_ssm[j], *ssm_w)
            conv_p.append(cb_p); ssm_p.append(h_p); conv_s.append(cb_s); ssm_s.append(h_s)
        else:
            lam_init = 0.8 - 0.6 * math.exp(-0.3 * i)
            lam = (jnp.exp(jnp.sum(attn_lambda_q1[j].astype(F32) * attn_lambda_k1[j].astype(F32)))
                   - jnp.exp(jnp.sum(attn_lambda_q2[j].astype(F32) * attn_lambda_k2[j].astype(F32))) + lam_init)
            q, k, v, g = diff_attn_project(yp, pos_p, attn_w_in[j])
            mp = diff_attn_output(diff_attn_prompt(q, k, v, lam), g, lam_init, attn_subln_w[j], attn_w_out[j])
            k_p.append(k); v_p.append(v)
            q, k, v, g = diff_attn_project(ys, pos_s, attn_w_in[j])
            att = diff_attn_sample(q, k, v, cache_k, cache_v, j, page_table, lam)
            ms = diff_attn_output(att, g, lam_init, attn_subln_w[j], attn_w_out[j])
            k_s.append(k); v_s.append(v)
        yp = layer_norm(DEEPNORM_ALPHA * yp + mp, ln_g[i], ln_b[i])
        ys = layer_norm(DEEPNORM_ALPHA * ys + ms, ln_g[i], ln_b[i])
    return (yp, ys, jnp.stack(conv_p), jnp.stack(ssm_p), jnp.stack(k_p), jnp.stack(v_p),
            jnp.stack(conv_s), jnp.stack(ssm_s), jnp.stack(k_s), jnp.stack(v_s))
```

```python
import functools
import math

import jax
import jax.numpy as jnp
from jax.experimental import pallas as pl
from jax.experimental.pallas import tpu as pltpu

F32 = jnp.float32
BF16 = jnp.bfloat16

NORM_EPS = 1e-5
ROPE_THETA = 10000.0
CONV_WIDTH = 4
SSM_CHUNK = 128
TAIL_ROWS = 8
LANES = 128
VMEM_LIMIT = 56 * 1024 * 1024


def _cparams(sem):
    return pltpu.CompilerParams(dimension_semantics=sem, vmem_limit_bytes=VMEM_LIMIT)


def _sigmoid(x):
    return 1.0 / (1.0 + jnp.exp(-x))


def _pick(n, pref):
    if n <= pref:
        return n
    t = pref
    while n % t:
        t //= 2
    return t


def _mm_kernel(x_ref, w_ref, o_ref, acc_ref, *, nk):
    part = jnp.dot(x_ref[...].astype(BF16), w_ref[...].astype(BF16), preferred_element_type=F32)
    if nk == 1:
        o_ref[...] = part.astype(o_ref.dtype)
        return
    k = pl.program_id(2)

    @pl.when(k == 0)
    def _():
        acc_ref[...] = part

    @pl.when(jnp.logical_and(k > 0, k < nk - 1))
    def _():
        acc_ref[...] += part

    @pl.when(k == nk - 1)
    def _():
        o_ref[...] = (acc_ref[...] + part).astype(o_ref.dtype)


def _matmul(x, w3, layer, *, n_off=0, n=None, tm=1024, tn=1024, tk=512, out_dtype=F32, name="matmul"):
    m, kdim = x.shape
    n = w3.shape[2] - n_off if n is None else n
    tm, tn, tk = _pick(m, tm), _pick(n, tn), _pick(kdim, tk)
    assert n_off % tn == 0
    nk = kdim // tk
    off = n_off // tn
    return pl.pallas_call(
        functools.partial(_mm_kernel, nk=nk),
        grid=(m // tm, n // tn, nk),
        in_specs=[pl.BlockSpec((tm, tk), lambda i, j, k: (i, k)),
                  pl.BlockSpec((None, tk, tn), lambda i, j, k: (layer, k, j + off))],
        out_specs=pl.BlockSpec((tm, tn), lambda i, j, k: (i, j)),
        out_shape=jax.ShapeDtypeStruct((m, n), out_dtype),
        scratch_shapes=[pltpu.VMEM((tm, tn), F32)],
        compiler_params=_cparams(("parallel", "parallel", "arbitrary")),
        name=name,
    )(x, w3)


def _add_ln_kernel(x_ref, m_ref, g_ref, b_ref, o_ref, ob_ref, *, alpha):
    v = alpha * x_ref[...] + m_ref[...]
    mu = jnp.mean(v, axis=-1, keepdims=True)
    d = v - mu
    var = jnp.mean(d * d, axis=-1, keepdims=True)
    y = d * jax.lax.rsqrt(var + NORM_EPS) * g_ref[...] + b_ref[...]
    o_ref[...] = y
    ob_ref[...] = y.astype(BF16)


def _add_ln(x, mix, g3, b3, layer, alpha, *, tm=256):
    m, d = x.shape
    tm = _pick(m, tm)
    row = pl.BlockSpec((tm, d), lambda i: (i, 0))
    par = pl.BlockSpec((None, 1, d), lambda i: (layer, 0, 0))
    return pl.pallas_call(
        functools.partial(_add_ln_kernel, alpha=alpha),
        grid=(m // tm,),
        in_specs=[row, row, par, par],
        out_specs=[row, row],
        out_shape=[jax.ShapeDtypeStruct((m, d), F32), jax.ShapeDtypeStruct((m, d), BF16)],
        compiler_params=_cparams(("parallel",)),
        name="add_ln",
    )(x, mix, g3, b3)


def _ssd_kernel(*refs, valid, hg, zero_init):
    if zero_init:
        (z_ref, x_ref, b_ref, c_ref, dt_ref, cwx_ref, cwb_ref, cwc_ref, cbx_ref, cbb_ref, cbc_ref,
         dtb_ref, alog_ref, dsk_ref, nw_ref, y_ref, h_ref, ext_x, ext_b, ext_c, h_t) = refs
    else:
        (z_ref, x_ref, b_ref, c_ref, dt_ref, cwx_ref, cwb_ref, cwc_ref, cbx_ref, cbb_ref, cbc_ref,
         dtb_ref, alog_ref, dsk_ref, nw_ref, bufx_ref, bufb_ref, bufc_ref, h0_ref,
         y_ref, h_ref, ext_x, ext_b, ext_c, h_t) = refs
    L = SSM_CHUNK
    g = pl.program_id(1)
    c = pl.program_id(2)
    nc = pl.num_programs(2)
    exts = (ext_x, ext_b, ext_c)

    @pl.when(c == 0)
    def _init():
        if zero_init:
            for e in exts:
                e[0:TAIL_ROWS, :] = jnp.zeros((TAIL_ROWS, e.shape[1]), F32)
            h_t[...] = jnp.zeros(h_t.shape, F32)
        else:
            for e, buf in zip(exts, (bufx_ref, bufb_ref, bufc_ref)):
                e[...] = jnp.zeros(e.shape, F32)
                e[TAIL_ROWS - (CONV_WIDTH - 1):TAIL_ROWS, :] = buf[...]
            h_t[...] = h0_ref[...].T

    for e, raw in zip(exts, (x_ref, b_ref, c_ref)):
        e[TAIL_ROWS:TAIL_ROWS + valid, :] = raw[...]

    def conv_silu(e, cw_ref, cb_ref):
        acc = cb_ref[...] + cw_ref[CONV_WIDTH - 1:CONV_WIDTH, :] * e[TAIL_ROWS:TAIL_ROWS + L, :]
        for s in range(1, CONV_WIDTH):
            acc = acc + cw_ref[CONV_WIDTH - 1 - s:CONV_WIDTH - s, :] * e[TAIL_ROWS - s:TAIL_ROWS - s + L, :]
        return acc * _sigmoid(acc)

    xs = conv_silu(ext_x, cwx_ref, cbx_ref)
    bm = conv_silu(ext_b, cwb_ref, cbb_ref)
    cm = conv_silu(ext_c, cwc_ref, cbc_ref)

    if valid == L:
        for e in exts:
            e[0:TAIL_ROWS, :] = e[L:L + TAIL_ROWS, :]

    dtv = dt_ref[...] + dtb_ref[...]
    dt = jnp.maximum(dtv, 0.0) + jnp.log1p(jnp.exp(-jnp.abs(dtv)))
    if valid < L:
        dt = jnp.concatenate([dt, jnp.zeros((L - valid, dt.shape[1]), F32)], axis=0)
    da = dt * (-jnp.exp(alog_ref[...]))
    rows = jax.lax.broadcasted_iota(jnp.int32, (L, L), 0)
    cols = jax.lax.broadcasted_iota(jnp.int32, (L, L), 1)
    causal = rows >= cols
    tril = jnp.where(causal, 1.0, 0.0).astype(BF16)
    p0 = da.astype(BF16)
    r1 = da - p0.astype(F32)
    p1 = r1.astype(BF16)
    p2 = (r1 - p1.astype(F32)).astype(BF16)
    a_cs = (jnp.dot(tril, p0, preferred_element_type=F32) + jnp.dot(tril, p1, preferred_element_type=F32)
            + jnp.dot(tril, p2, preferred_element_type=F32))

    nh = dt.shape[1]
    shift = jax.lax.rem(nh - g * hg, nh)
    a_g = pltpu.roll(a_cs, shift, 1)
    dt_g = pltpu.roll(dt, shift, 1)
    dsk_g = pltpu.roll(dsk_ref[...], shift, 1)
    a_gt = a_g.T

    cmb = cm.astype(BF16)
    cb = jax.lax.dot_general(cmb, bm.astype(BF16), (((1,), (1,)), ((), ())), preferred_element_type=F32)
    y_off = jnp.dot(cmb, h_t[...].astype(BF16), preferred_element_type=F32)

    lane = jax.lax.broadcasted_iota(jnp.int32, (L, LANES), 1)
    lo_half = lane < (LANES // 2)
    lo_half1 = lo_half[0:1, :]
    ys, xdds, cds = [], [], []
    for q in range(hg // 2):
        j0, j1 = 2 * q, 2 * q + 1
        col0 = jnp.broadcast_to(a_g[:, j0:j0 + 1], (L, LANES))
        col1 = jnp.broadcast_to(a_g[:, j1:j1 + 1], (L, LANES))
        m0 = (cb * jnp.exp(jnp.where(causal, col0 - a_gt[j0:j0 + 1, :], -jnp.inf))).astype(BF16)
        m1 = (cb * jnp.exp(jnp.where(causal, col1 - a_gt[j1:j1 + 1, :], -jnp.inf))).astype(BF16)
        dtp = jnp.where(lo_half, jnp.broadcast_to(dt_g[:, j0:j0 + 1], (L, LANES)),
                        jnp.broadcast_to(dt_g[:, j1:j1 + 1], (L, LANES)))
        colp = jnp.where(lo_half, col0, col1)
        dskp = jnp.where(lo_half1, jnp.broadcast_to(dsk_g[:, j0:j0 + 1], (1, LANES)),
                         jnp.broadcast_to(dsk_g[:, j1:j1 + 1], (1, LANES)))
        xs_p = xs[:, q * LANES:(q + 1) * LANES]
        xd_p = xs_p * dtp
        xd_b = xd_p.astype(BF16)
        y_diag = jnp.where(lo_half, jnp.dot(m0, xd_b, preferred_element_type=F32),
                           jnp.dot(m1, xd_b, preferred_element_type=F32))
        e_p = jnp.exp(colp)
        ys.append(y_diag + y_off[:, q * LANES:(q + 1) * LANES] * e_p + dskp * xs_p)
        xdds.append((xd_p * jnp.exp(colp[L - 1:L, :] - colp)).astype(BF16))
        cds.append(e_p[L - 1:L, :])
    y = jnp.concatenate(ys, axis=1)[0:valid, :]
    xdd = jnp.concatenate(xdds, axis=1)
    chunk_decay = jnp.concatenate(cds, axis=1)

    s_t = jnp.dot(bm.T.astype(BF16), xdd, preferred_element_type=F32)
    h_t[...] = h_t[...] * chunk_decay + s_t

    zg = z_ref[...]
    y = y * (zg * _sigmoid(zg))
    y = y * jax.lax.rsqrt(jnp.mean(y * y, axis=-1, keepdims=True) + NORM_EPS) * nw_ref[...]
    y_ref[...] = y.astype(y_ref.dtype)

    @pl.when(c == nc - 1)
    def _fin():
        h_ref[...] = h_t[...].T


def _ssd(zx, dt_raw, prm, layer, *, out_dtype, conv_buf=None, h0=None):
    conv_w, conv_b, dt_bias, a_log, d_skip, norm_w = prm
    nb, t, nh = dt_raw.shape
    d_inner = norm_w.shape[2]
    conv_dim = conv_w.shape[2]
    gn = (conv_dim - d_inner) // 2
    n = LANES
    groups = gn // n
    hg = nh // groups
    gw = d_inner // groups
    zero_init = conv_buf is None
    L = SSM_CHUNK
    if t >= L:
        assert t % L == 0
        valid, nc = L, t // L
    else:
        assert t % TAIL_ROWS == 0
        valid, nc = t, 1
    xo, bo, co = d_inner // gw, 2 * d_inner // n, (2 * d_inner + gn) // n

    def rowblk(width, off):
        return pl.BlockSpec((None, valid, width), lambda b, g, c: (b, c, off + g))

    def par3(rows, width, off):
        return pl.BlockSpec((None, rows, width), lambda b, g, c: (layer, 0, off + g))

    full = pl.BlockSpec((None, 1, nh), lambda b, g, c: (layer, 0, 0))
    in_specs = [rowblk(gw, 0), rowblk(gw, xo), rowblk(n, bo), rowblk(n, co),
                pl.BlockSpec((None, valid, nh), lambda b, g, c: (b, c, 0)),
                par3(CONV_WIDTH, gw, 0), par3(CONV_WIDTH, n, d_inner // n), par3(CONV_WIDTH, n, (d_inner + gn) // n),
                par3(1, gw, 0), par3(1, n, d_inner // n), par3(1, n, (d_inner + gn) // n),
                full, full, full, par3(1, gw, 0)]
    args = [zx, zx, zx, zx, dt_raw, conv_w, conv_w, conv_w, conv_b, conv_b, conv_b, dt_bias, a_log, d_skip, norm_w]
    if not zero_init:
        def bufblk(width, off):
            return pl.BlockSpec((None, CONV_WIDTH - 1, width), lambda b, g, c: (b, 0, off + g))
        in_specs += [bufblk(gw, 0), bufblk(n, d_inner // n), bufblk(n, (d_inner + gn) // n),
                     pl.BlockSpec((None, gw, n), lambda b, g, c: (b, g, 0))]
        args += [conv_buf, conv_buf, conv_buf, h0]
    y, h = pl.pallas_call(
        functools.partial(_ssd_kernel, valid=valid, hg=hg, zero_init=zero_init),
        grid=(nb, groups, nc),
        in_specs=in_specs,
        out_specs=[rowblk(gw, 0),
                   pl.BlockSpec((None, gw, n), lambda b, g, c: (b, g, 0))],
        out_shape=[jax.ShapeDtypeStruct((nb, t, d_inner), out_dtype),
                   jax.ShapeDtypeStruct((nb, d_inner, n), F32)],
        scratch_shapes=[pltpu.VMEM((L + TAIL_ROWS, gw), F32), pltpu.VMEM((L + TAIL_ROWS, n), F32),
                        pltpu.VMEM((L + TAIL_ROWS, n), F32), pltpu.VMEM((n, gw), F32)],
        compiler_params=_cparams(("parallel", "parallel", "arbitrary")),
        name="ssd",
    )(*args)
    return y, h


def _rope_kernel(x_ref, cos_ref, sin_ref, o_ref, *, scale):
    cos = cos_ref[...]
    sin = sin_ref[...]
    for h in range(x_ref.shape[1] // LANES):
        x = x_ref[:, h * LANES:(h + 1) * LANES]
        r = x * cos + pltpu.roll(x, LANES // 2, 1) * sin
        if scale != 1.0:
            r = r * scale
        o_ref[:, h * LANES:(h + 1) * LANES] = r.astype(o_ref.dtype)


def _rope(qkvg, cos, sin, *, col_off, width, scale, out_dtype, tm=256, tn=1024):
    m = qkvg.shape[0]
    tm, tn = _pick(m, tm), _pick(width, tn)
    off = col_off // tn
    tab = pl.BlockSpec((tm, LANES), lambda i, j: (i, 0))
    return pl.pallas_call(
        functools.partial(_rope_kernel, scale=scale),
        grid=(m // tm, width // tn),
        in_specs=[pl.BlockSpec((tm, tn), lambda i, j: (i, j + off)), tab, tab],
        out_specs=pl.BlockSpec((tm, tn), lambda i, j: (i, j)),
        out_shape=jax.ShapeDtypeStruct((m, width), out_dtype),
        compiler_params=_cparams(("parallel", "parallel")),
        name="rope",
    )(qkvg, cos, sin)


def _lambda_value(lam_ref, lam_init):
    s1 = jnp.sum(lam_ref[0:1, :] * lam_ref[1:2, :], axis=-1, keepdims=True)
    s2 = jnp.sum(lam_ref[2:3, :] * lam_ref[3:4, :], axis=-1, keepdims=True)
    return jnp.exp(s1) - jnp.exp(s2) + lam_init


def _diff_finish(acc1, l1, acc2, l2, lam, g, subw, lam_init):
    o = acc1 / l1 - lam * (acc2 / l2)
    o = o * jax.lax.rsqrt(jnp.mean(o * o, axis=-1, keepdims=True) + NORM_EPS) * subw * (1.0 - lam_init)
    return o * (g * _sigmoid(g))


def _flash_kernel(q_ref, k_ref, v_ref, g_ref, lam_ref, subw_ref, o_ref, m_ref, l_ref, acc_ref, *, dh, lam_init):
    qi = pl.program_id(2)
    ki = pl.program_id(3)
    tq, tk = q_ref.shape[0], k_ref.shape[0]

    @pl.when(ki == 0)
    def _init():
        m_ref[...] = jnp.full(m_ref.shape, -jnp.inf, F32)
        l_ref[...] = jnp.zeros(l_ref.shape, F32)
        acc_ref[...] = jnp.zeros(acc_ref.shape, F32)

    def step(masked):
        vb = v_ref[...].astype(BF16)
        if masked:
            keep = (jax.lax.broadcasted_iota(jnp.int32, (tq, tk), 0)
                    >= jax.lax.broadcasted_iota(jnp.int32, (tq, tk), 1))
        for c in range(2):
            q = q_ref[:, c * dh:(c + 1) * dh]
            k = k_ref[:, c * dh:(c + 1) * dh].astype(BF16)
            s = jax.lax.dot_general(q, k, (((1,), (1,)), ((), ())), preferred_element_type=F32)
            if masked:
                s = jnp.where(keep, s, -jnp.inf)
            m_old = m_ref[c]
            m_new = jnp.maximum(m_old, jnp.max(s, axis=-1, keepdims=True))
            p = jnp.exp(s - m_new)
            corr = jnp.exp(m_old - m_new)
            l_ref[c] = l_ref[c] * corr + jnp.sum(p, axis=-1, keepdims=True)
            acc_ref[c] = acc_ref[c] * corr + jnp.dot(p.astype(BF16), vb, preferred_element_type=F32)
            m_ref[c] = m_new

    @pl.when(ki < qi)
    def _full():
        step(False)

    @pl.when(ki == qi)
    def _diag():
        step(True)
        lam = _lambda_value(lam_ref, lam_init)
        o = _diff_finish(acc_ref[0], l_ref[0], acc_ref[1], l_ref[1], lam, g_ref[...], subw_ref[...], lam_init)
        o_ref[...] = o.astype(o_ref.dtype)


def _flash(q, k, qkvg, lam4, subw3, layer, *, nb, t, heads, dh, lam_init, tq=512):
    dv = 2 * dh
    tq = _pick(t, tq)
    nq = t // tq
    v_off = (2 * heads * 2 * dh) // dv
    g_off = v_off + heads
    qspec = pl.BlockSpec((tq, dv), lambda b, h, i, j: (b * nq + i, h))
    return pl.pallas_call(
        functools.partial(_flash_kernel, dh=dh, lam_init=lam_init),
        grid=(nb, heads, nq, nq),
        in_specs=[qspec,
                  pl.BlockSpec((tq, dv), lambda b, h, i, j: (b * nq + jnp.minimum(i, j), h)),
                  pl.BlockSpec((tq, dv), lambda b, h, i, j: (b * nq + jnp.minimum(i, j), v_off + h)),
                  pl.BlockSpec((tq, dv), lambda b, h, i, j: (b * nq + i, g_off + h)),
                  pl.BlockSpec((None, 4, dh), lambda b, h, i, j: (layer, 0, 0)),
                  pl.BlockSpec((None, 1, dv), lambda b, h, i, j: (layer, 0, 0))],
        out_specs=qspec,
        out_shape=jax.ShapeDtypeStruct((nb * t, heads * dv), BF16),
        scratch_shapes=[pltpu.VMEM((2, tq, 1), F32), pltpu.VMEM((2, tq, 1), F32), pltpu.VMEM((2, tq, dv), F32)],
        compiler_params=_cparams(("parallel", "parallel", "parallel", "arbitrary")),
        name="flash_diff_attn",
    )(q, k, qkvg, qkvg, lam4, subw3)


def _paged_kernel(pt_ref, q_ref, kc_ref, vc_ref, kn_ref, vn_ref, g_ref, lam_ref, subw_ref, o_ref,
                  m_ref, l_ref, acc_ref, *, heads, dh, lam_init):
    p = pl.program_id(1)
    n_pages = pl.num_programs(1) - 1
    dv = 2 * dh
    t = g_ref.shape[0]
    tp = q_ref.shape[0]
    page = kc_ref.shape[0]

    @pl.when(p == 0)
    def _init():
        m_ref[...] = jnp.full(m_ref.shape, -jnp.inf, F32)
        l_ref[...] = jnp.zeros(l_ref.shape, F32)
        acc_ref[...] = jnp.zeros(acc_ref.shape, F32)

    def block(k_ref, v_ref, keep):
        for h in range(heads):
            ss = []
            for c in range(2):
                lo = (2 * h + c) * dh
                s = jax.lax.dot_general(q_ref[:, lo:lo + dh], k_ref[:, lo:lo + dh].astype(BF16),
                                        (((1,), (1,)), ((), ())), preferred_element_type=F32)
                ss.append(s if keep is None else jnp.where(keep, s, -jnp.inf))
            s = jnp.concatenate(ss, axis=0)
            m_old = m_ref[h]
            m_new = jnp.maximum(m_old, jnp.max(s, axis=-1, keepdims=True))
            pr = jnp.exp(s - m_new)
            corr = jnp.exp(m_old - m_new)
            l_ref[h] = l_ref[h] * corr + jnp.sum(pr, axis=-1, keepdims=True)
            acc_ref[h] = acc_ref[h] * corr + jnp.dot(pr.astype(BF16), v_ref[:, h * dv:(h + 1) * dv].astype(BF16),
                                                     preferred_element_type=F32)
            m_ref[h] = m_new

    @pl.when(p < n_pages)
    def _cache():
        block(kc_ref, vc_ref, None)

    @pl.when(p == n_pages)
    def _new():
        rows = jax.lax.broadcasted_iota(jnp.int32, (tp, page), 0)
        cols = jax.lax.broadcasted_iota(jnp.int32, (tp, page), 1)
        block(kn_ref, vn_ref, cols <= jnp.minimum(rows, t - 1))
        lam = _lambda_value(lam_ref, lam_init)
        for h in range(heads):
            acc, l = acc_ref[h], l_ref[h]
            o = _diff_finish(acc[0:t], l[0:t], acc[tp:tp + t], l[tp:tp + t], lam,
                             g_ref[:, h * dv:(h + 1) * dv], subw_ref[...], lam_init)
            o_ref[:, h * dv:(h + 1) * dv] = o.astype(o_ref.dtype)


def _paged(q, cache_k4, cache_v4, k_new, v_new, qkvg, page_table, lam4, subw3, layer, *, heads, dh, lam_init):
    nb, n_pages = page_table.shape
    page = cache_k4.shape[2]
    width = cache_k4.shape[3]
    dv = 2 * dh
    t = qkvg.shape[1]
    tp = q.shape[1]
    g_off = (2 * width + heads * dv) // width
    last = n_pages - 1

    def cache_spec():
        return pl.BlockSpec((None, None, page, width),
                            lambda b, p, pt: (layer, pt[b * n_pages + jnp.minimum(p, last)], 0, 0))

    def rows3(r, off):
        return pl.BlockSpec((None, r, width), lambda b, p, pt: (b, 0, off))

    return pl.pallas_call(
        functools.partial(_paged_kernel, heads=heads, dh=dh, lam_init=lam_init),
        grid_spec=pltpu.PrefetchScalarGridSpec(
            num_scalar_prefetch=1,
            grid=(nb, n_pages + 1),
            in_specs=[rows3(tp, 0), cache_spec(), cache_spec(), rows3(page, 0), rows3(page, 0), rows3(t, g_off),
                      pl.BlockSpec((None, 4, dh), lambda b, p, pt: (layer, 0, 0)),
                      pl.BlockSpec((None, 1, dv), lambda b, p, pt: (layer, 0, 0))],
            out_specs=rows3(t, 0),
            scratch_shapes=[pltpu.VMEM((heads, 2 * tp, 1), F32), pltpu.VMEM((heads, 2 * tp, 1), F32),
                            pltpu.VMEM((heads, 2 * tp, dv), F32)]),
        out_shape=jax.ShapeDtypeStruct((nb, t, width), F32),
        compiler_params=_cparams(("parallel", "arbitrary")),
        name="paged_diff_attn",
    )(page_table.reshape(-1), q, cache_k4, cache_v4, k_new, v_new, qkvg, lam4, subw3)


def _rope_tables(pos, dh):
    half = dh // 2
    inv = ROPE_THETA ** (-jnp.arange(half, dtype=F32) / half)
    ang = pos.astype(F32)[:, None] * inv[None]
    cos, sin = jnp.cos(ang), jnp.sin(ang)
    return jnp.concatenate([cos, cos], -1), jnp.concatenate([-sin, sin], -1)


def kernel(x_prompt, x_sample, state_conv, state_ssm, cache_k, cache_v, page_table, ssm_w_in, ssm_conv_w, ssm_conv_b, ssm_dt_bias, ssm_a_log, ssm_d, ssm_norm_w, ssm_w_out, attn_w_in, attn_lambda_q1, attn_lambda_k1, attn_lambda_q2, attn_lambda_k2, attn_subln_w, attn_w_out, ln_g, ln_b):
    bp, tp, d_model = x_prompt.shape
    bs, ts, _ = x_sample.shape
    depth = ln_g.shape[0]
    n_mixers = 2
    alpha = (2 * depth) ** 0.25
    past_len = page_table.shape[1] * cache_k.shape[2]

    _, _, nh, hp, n_state = state_ssm.shape
    d_inner = nh * hp
    conv_dim = state_conv.shape[-1]
    zx_w = d_inner + conv_dim
    n_pool, page, kheads, dh = cache_k.shape[1:]
    heads = kheads // 2
    dv = 2 * dh
    qk_dim = kheads * dh
    width = heads * dv
    assert n_state == LANES and dh == LANES and 2 * hp == LANES and qk_dim == width

    ln_g3, ln_b3 = ln_g.reshape(depth, 1, d_model), ln_b.reshape(depth, 1, d_model)
    ssm_prm = (ssm_conv_w, ssm_conv_b.reshape(-1, 1, conv_dim), ssm_dt_bias.reshape(-1, 1, nh),
               ssm_a_log.reshape(-1, 1, nh), ssm_d.reshape(-1, 1, nh), ssm_norm_w.reshape(-1, 1, d_inner))
    lam4 = jnp.stack([attn_lambda_q1, attn_lambda_k1, attn_lambda_q2, attn_lambda_k2], axis=1)
    subw3 = attn_subln_w.reshape(-1, 1, dv)
    cache_k4 = cache_k.reshape(cache_k.shape[0], n_pool, page, qk_dim)
    cache_v4 = cache_v.reshape(cache_v.shape[0], n_pool, page, width)
    state_ssm4 = state_ssm.reshape(state_ssm.shape[0], bs, d_inner, n_state)

    cos_p, sin_p = _rope_tables(jnp.tile(jnp.arange(tp), bp), dh)
    cos_s, sin_s = _rope_tables(jnp.tile(past_len + jnp.arange(ts), bs), dh)

    yp = x_prompt.reshape(bp * tp, d_model)
    ys = x_sample.reshape(bs * ts, d_model)
    ypb, ysb = yp.astype(BF16), ys.astype(BF16)
    conv_p, ssm_p, k_p, v_p, conv_s, ssm_s, k_s, v_s = ([] for _ in range(8))

    def ssm_layer(xb, j, nb, t, conv_buf, h0):
        zx = _matmul(xb, ssm_w_in, j, n=zx_w, name="ssm_in_proj").reshape(nb, t, zx_w)
        dt_raw = _matmul(xb, ssm_w_in, j, n_off=zx_w, n=nh, name="ssm_dt_proj").reshape(nb, t, nh)
        y, h = _ssd(zx, dt_raw, ssm_prm, j, out_dtype=BF16 if t >= SSM_CHUNK else F32, conv_buf=conv_buf, h0=h0)
        mix = _matmul(y.reshape(nb * t, d_inner), ssm_w_out, j, name="ssm_out_proj")
        assert t >= CONV_WIDTH - 1
        return mix, zx[:, t - (CONV_WIDTH - 1):, d_inner:], h.reshape(nb, nh, hp, n_state)

    def attn_project(xb, j, cos, sin):
        qkvg = _matmul(xb, attn_w_in, j, name="attn_in_proj")
        q = _rope(qkvg, cos, sin, col_off=0, width=qk_dim, scale=dh ** -0.5, out_dtype=BF16)
        k = _rope(qkvg, cos, sin, col_off=qk_dim, width=qk_dim, scale=1.0, out_dtype=F32)
        return qkvg, q, k, qkvg[:, 2 * qk_dim:2 * qk_dim + width]

    for i in range(depth):
        j = i // n_mixers
        if i % n_mixers == 0:
            mp, cb, h = ssm_layer(ypb, j, bp, tp, None, None)
            conv_p.append(cb); ssm_p.append(h)
            ms, cb, h = ssm_layer(ysb, j, bs, ts, state_conv[j], state_ssm4[j])
            conv_s.append(cb); ssm_s.append(h)
        else:
            lam_init = 0.8 - 0.6 * math.exp(-0.3 * i)
            qkvg, q, k, v = attn_project(ypb, j, cos_p, sin_p)
            og = _flash(q, k, qkvg, lam4, subw3, j, nb=bp, t=tp, heads=heads, dh=dh, lam_init=lam_init)
            mp = _matmul(og, attn_w_out, j, name="attn_out_proj")
            k_p.append(k.reshape(bp, tp, kheads, dh)); v_p.append(v.reshape(bp, tp, heads, dv))

            qkvg, q, k, v = attn_project(ysb, j, cos_s, sin_s)
            pad = lambda a, r: jnp.pad(a.reshape(bs, ts, width), ((0, 0), (0, r - ts), (0, 0)))
            og = _paged(pad(q, 2 * TAIL_ROWS), cache_k4, cache_v4, pad(k, page), pad(v, page),
                        qkvg.reshape(bs, ts, 4 * width), page_table, lam4, subw3, j,
                        heads=heads, dh=dh, lam_init=lam_init)
            ms = _matmul(og.reshape(bs * ts, width), attn_w_out, j, name="attn_out_proj")
            k_s.append(k.reshape(bs, ts, kheads, dh)); v_s.append(v.reshape(bs, ts, heads, dv))
        yp, ypb = _add_ln(yp, mp, ln_g3, ln_b3, i, alpha)
        ys, ysb = _add_ln(ys, ms, ln_g3, ln_b3, i, alpha)

    return (yp.reshape(bp, tp, d_model), ys.reshape(bs, ts, d_model),
            jnp.stack(conv_p), jnp.stack(ssm_p), jnp.stack(k_p), jnp.stack(v_p),
            jnp.stack(conv_s), jnp.stack(ssm_s), jnp.stack(k_s), jnp.stack(v_s))
```

```python
import functools
import math

import jax
import jax.numpy as jnp
from jax.experimental import pallas as pl
from jax.experimental.pallas import tpu as pltpu

F32 = jnp.float32
BF16 = jnp.bfloat16

NORM_EPS = 1e-5
ROPE_THETA = 10000.0
CONV_WIDTH = 4
SSM_CHUNK = 128
TAIL_ROWS = 8
LANES = 128
MAP_TILE = 8
VMEM_LIMIT = 56 * 1024 * 1024


def _cparams(sem):
    return pltpu.CompilerParams(dimension_semantics=sem, vmem_limit_bytes=VMEM_LIMIT)


def _sigmoid(x):
    return 1.0 / (1.0 + jnp.exp(-x))


def _pick(n, pref):
    if n <= pref:
        return n
    t = pref
    while n % t:
        t //= 2
    return t


def _mm_kernel(x_ref, w_ref, o_ref):
    o_ref[...] = jnp.dot(x_ref[...].astype(BF16), w_ref[...].astype(BF16),
                         preferred_element_type=F32).astype(o_ref.dtype)


def _mm_rope_kernel(x_ref, w_ref, cos_ref, sin_ref, o_ref, *, scale):
    y = jnp.dot(x_ref[...].astype(BF16), w_ref[...].astype(BF16), preferred_element_type=F32)
    cos = cos_ref[...]
    sin = sin_ref[...]
    for h in range(y.shape[1] // LANES):
        yh = y[:, h * LANES:(h + 1) * LANES]
        r = yh * cos + pltpu.roll(yh, LANES // 2, 1) * sin
        if scale != 1.0:
            r = r * scale
        o_ref[:, h * LANES:(h + 1) * LANES] = r.astype(o_ref.dtype)


RESIDENT_PANEL_BYTES = 16 * 1024 * 1024


def _mm_tiles(m, kdim, n, x_bytes, rope):
    if m <= 256:
        return m, _pick(n, 512), 2
    tm = _pick(m, 2048)
    if tm * kdim * x_bytes > RESIDENT_PANEL_BYTES:
        return tm // 2, _pick(n, 256), 1
    return (tm // 2 if rope else tm), _pick(n, 256), 2


def _matmul(x, w3, layer, *, n_off=0, n=None, out_dtype=F32, rope=None, name="matmul"):
    m, kdim = x.shape
    n = w3.shape[2] - n_off if n is None else n
    tm, tn, x_buffers = _mm_tiles(m, kdim, n, x.dtype.itemsize, rope is not None)
    assert n_off % tn == 0
    off = n_off // tn
    x_mode = {} if x_buffers == 2 else {"pipeline_mode": pl.Buffered(x_buffers)}
    in_specs = [pl.BlockSpec((tm, kdim), lambda i, j: (i, 0), **x_mode),
                pl.BlockSpec((None, kdim, tn), lambda i, j: (layer, 0, j + off))]
    args = [x, w3]
    body = _mm_kernel
    if rope is not None:
        cos, sin, scale = rope
        tab = pl.BlockSpec((tm, LANES), lambda i, j: (i, 0))
        in_specs += [tab, tab]
        args += [cos, sin]
        body = functools.partial(_mm_rope_kernel, scale=scale)
    return pl.pallas_call(
        body,
        grid=(m // tm, n // tn),
        in_specs=in_specs,
        out_specs=pl.BlockSpec((tm, tn), lambda i, j: (i, j)),
        out_shape=jax.ShapeDtypeStruct((m, n), out_dtype),
        compiler_params=_cparams(("parallel", "parallel")),
        name=name,
    )(*args)


def _add_ln_kernel(x_ref, m_ref, g_ref, b_ref, o_ref, ob_ref, *, alpha):
    v = alpha * x_ref[...] + m_ref[...]
    mu = jnp.mean(v, axis=-1, keepdims=True)
    d = v - mu
    var = jnp.mean(d * d, axis=-1, keepdims=True)
    y = d * jax.lax.rsqrt(var + NORM_EPS) * g_ref[...] + b_ref[...]
    o_ref[...] = y
    ob_ref[...] = y.astype(BF16)


def _add_ln(x, mix, g3, b3, layer, alpha, *, tm=256):
    m, d = x.shape
    tm = _pick(m, tm)
    row = pl.BlockSpec((tm, d), lambda i: (i, 0))
    par = pl.BlockSpec((None, 1, d), lambda i: (layer, 0, 0))
    return pl.pallas_call(
        functools.partial(_add_ln_kernel, alpha=alpha),
        grid=(m // tm,),
        in_specs=[row, row, par, par],
        out_specs=[row, row],
        out_shape=[jax.ShapeDtypeStruct((m, d), F32), jax.ShapeDtypeStruct((m, d), BF16)],
        compiler_params=_cparams(("parallel",)),
        name="add_ln",
    )(x, mix, g3, b3)


def _ssd_kernel(*refs, valid, hg, zero_init):
    if zero_init:
        (z_ref, x_ref, b_ref, c_ref, dt_ref, cwx_ref, cwb_ref, cwc_ref, cbx_ref, cbb_ref, cbc_ref,
         dtb_ref, alog_ref, dsk_ref, nw_ref, y_ref, h_ref, ext_x, ext_b, ext_c, h_t) = refs
    else:
        (z_ref, x_ref, b_ref, c_ref, dt_ref, cwx_ref, cwb_ref, cwc_ref, cbx_ref, cbb_ref, cbc_ref,
         dtb_ref, alog_ref, dsk_ref, nw_ref, bufx_ref, bufb_ref, bufc_ref, h0_ref,
         y_ref, h_ref, ext_x, ext_b, ext_c, h_t) = refs
    L = SSM_CHUNK
    g = pl.program_id(1)
    c = pl.program_id(2)
    nc = pl.num_programs(2)
    exts = (ext_x, ext_b, ext_c)

    @pl.when(c == 0)
    def _init():
        if zero_init:
            for e in exts:
                e[0:TAIL_ROWS, :] = jnp.zeros((TAIL_ROWS, e.shape[1]), F32)
            h_t[...] = jnp.zeros(h_t.shape, F32)
        else:
            for e, buf in zip(exts, (bufx_ref, bufb_ref, bufc_ref)):
                e[...] = jnp.zeros(e.shape, F32)
                e[TAIL_ROWS - (CONV_WIDTH - 1):TAIL_ROWS, :] = buf[...]
            h_t[...] = h0_ref[...].T

    for e, raw in zip(exts, (x_ref, b_ref, c_ref)):
        e[TAIL_ROWS:TAIL_ROWS + valid, :] = raw[...]

    def conv_silu(e, cw_ref, cb_ref):
        acc = cb_ref[...] + cw_ref[CONV_WIDTH - 1:CONV_WIDTH, :] * e[TAIL_ROWS:TAIL_ROWS + L, :]
        for s in range(1, CONV_WIDTH):
            acc = acc + cw_ref[CONV_WIDTH - 1 - s:CONV_WIDTH - s, :] * e[TAIL_ROWS - s:TAIL_ROWS - s + L, :]
        return acc * _sigmoid(acc)

    xs = conv_silu(ext_x, cwx_ref, cbx_ref)
    bm = conv_silu(ext_b, cwb_ref, cbb_ref)
    cm = conv_silu(ext_c, cwc_ref, cbc_ref)

    if valid == L:
        for e in exts:
            e[0:TAIL_ROWS, :] = e[L:L + TAIL_ROWS, :]

    dtv = dt_ref[...] + dtb_ref[...]
    dt = jnp.maximum(dtv, 0.0) + jnp.log1p(jnp.exp(-jnp.abs(dtv)))
    if valid < L:
        dt = jnp.concatenate([dt, jnp.zeros((L - valid, dt.shape[1]), F32)], axis=0)
    da = dt * (-jnp.exp(alog_ref[...]))
    rows = jax.lax.broadcasted_iota(jnp.int32, (L, L), 0)
    cols = jax.lax.broadcasted_iota(jnp.int32, (L, L), 1)
    causal = rows >= cols
    tril = jnp.where(causal, 1.0, 0.0).astype(BF16)
    p0 = da.astype(BF16)
    r1 = da - p0.astype(F32)
    p1 = r1.astype(BF16)
    p2 = (r1 - p1.astype(F32)).astype(BF16)
    a_cs = (jnp.dot(tril, p0, preferred_element_type=F32) + jnp.dot(tril, p1, preferred_element_type=F32)
            + jnp.dot(tril, p2, preferred_element_type=F32))

    nh = dt.shape[1]
    shift = jax.lax.rem(nh - g * hg, nh)
    a_g = pltpu.roll(a_cs, shift, 1)
    dt_g = pltpu.roll(dt, shift, 1)
    dsk_g = pltpu.roll(dsk_ref[...], shift, 1)
    a_gt = a_g.T

    cmb = cm.astype(BF16)
    cb = jax.lax.dot_general(cmb, bm.astype(BF16), (((1,), (1,)), ((), ())), preferred_element_type=F32)
    y_off = jnp.dot(cmb, h_t[...].astype(BF16), preferred_element_type=F32)

    lane = jax.lax.broadcasted_iota(jnp.int32, (L, LANES), 1)
    lo_half = lane < (LANES // 2)
    lo_half1 = lo_half[0:1, :]
    ys, xdds, cds = [], [], []
    for q in range(hg // 2):
        j0, j1 = 2 * q, 2 * q + 1
        col0 = jnp.broadcast_to(a_g[:, j0:j0 + 1], (L, LANES))
        col1 = jnp.broadcast_to(a_g[:, j1:j1 + 1], (L, LANES))
        m0 = (cb * jnp.exp(jnp.where(causal, col0 - a_gt[j0:j0 + 1, :], -jnp.inf))).astype(BF16)
        m1 = (cb * jnp.exp(jnp.where(causal, col1 - a_gt[j1:j1 + 1, :], -jnp.inf))).astype(BF16)
        dtp = jnp.where(lo_half, jnp.broadcast_to(dt_g[:, j0:j0 + 1], (L, LANES)),
                        jnp.broadcast_to(dt_g[:, j1:j1 + 1], (L, LANES)))
        colp = jnp.where(lo_half, col0, col1)
        dskp = jnp.where(lo_half1, jnp.broadcast_to(dsk_g[:, j0:j0 + 1], (1, LANES)),
                         jnp.broadcast_to(dsk_g[:, j1:j1 + 1], (1, LANES)))
        xs_p = xs[:, q * LANES:(q + 1) * LANES]
        xd_p = xs_p * dtp
        xd_b = xd_p.astype(BF16)
        y_diag = jnp.where(lo_half, jnp.dot(m0, xd_b, preferred_element_type=F32),
                           jnp.dot(m1, xd_b, preferred_element_type=F32))
        e_p = jnp.exp(colp)
        ys.append(y_diag + y_off[:, q * LANES:(q + 1) * LANES] * e_p + dskp * xs_p)
        xdds.append((xd_p * jnp.exp(colp[L - 1:L, :] - colp)).astype(BF16))
        cds.append(e_p[L - 1:L, :])
    y = jnp.concatenate(ys, axis=1)[0:valid, :]
    xdd = jnp.concatenate(xdds, axis=1)
    chunk_decay = jnp.concatenate(cds, axis=1)

    s_t = jnp.dot(bm.T.astype(BF16), xdd, preferred_element_type=F32)
    h_t[...] = h_t[...] * chunk_decay + s_t

    zg = z_ref[...]
    y = y * (zg * _sigmoid(zg))
    y = y * jax.lax.rsqrt(jnp.mean(y * y, axis=-1, keepdims=True) + NORM_EPS) * nw_ref[...]
    y_ref[...] = y.astype(y_ref.dtype)

    @pl.when(c == nc - 1)
    def _fin():
        h_ref[...] = h_t[...].T


def _ssd(zx, dt_raw, prm, layer, *, out_dtype, conv_buf=None, h0=None):
    conv_w, conv_b, dt_bias, a_log, d_skip, norm_w = prm
    nb, t, nh = dt_raw.shape
    d_inner = norm_w.shape[2]
    conv_dim = conv_w.shape[2]
    gn = (conv_dim - d_inner) // 2
    n = LANES
    groups = gn // n
    hg = nh // groups
    gw = d_inner // groups
    zero_init = conv_buf is None
    L = SSM_CHUNK
    if t >= L:
        assert t % L == 0
        valid, nc = L, t // L
    else:
        assert t % TAIL_ROWS == 0
        valid, nc = t, 1
    xo, bo, co = d_inner // gw, 2 * d_inner // n, (2 * d_inner + gn) // n

    def rowblk(width, off):
        return pl.BlockSpec((None, valid, width), lambda b, g, c: (b, c, off + g))

    def par3(rows, width, off):
        return pl.BlockSpec((None, rows, width), lambda b, g, c: (layer, 0, off + g))

    full = pl.BlockSpec((None, 1, nh), lambda b, g, c: (layer, 0, 0))
    in_specs = [rowblk(gw, 0), rowblk(gw, xo), rowblk(n, bo), rowblk(n, co),
                pl.BlockSpec((None, valid, nh), lambda b, g, c: (b, c, 0)),
                par3(CONV_WIDTH, gw, 0), par3(CONV_WIDTH, n, d_inner // n), par3(CONV_WIDTH, n, (d_inner + gn) // n),
                par3(1, gw, 0), par3(1, n, d_inner // n), par3(1, n, (d_inner + gn) // n),
                full, full, full, par3(1, gw, 0)]
    args = [zx, zx, zx, zx, dt_raw, conv_w, conv_w, conv_w, conv_b, conv_b, conv_b, dt_bias, a_log, d_skip, norm_w]
    if not zero_init:
        def bufblk(width, off):
            return pl.BlockSpec((None, CONV_WIDTH - 1, width), lambda b, g, c: (b, 0, off + g))
        in_specs += [bufblk(gw, 0), bufblk(n, d_inner // n), bufblk(n, (d_inner + gn) // n),
                     pl.BlockSpec((None, gw, n), lambda b, g, c: (b, g, 0))]
        args += [conv_buf, conv_buf, conv_buf, h0]
    y, h = pl.pallas_call(
        functools.partial(_ssd_kernel, valid=valid, hg=hg, zero_init=zero_init),
        grid=(nb, groups, nc),
        in_specs=in_specs,
        out_specs=[rowblk(gw, 0),
                   pl.BlockSpec((None, gw, n), lambda b, g, c: (b, g, 0))],
        out_shape=[jax.ShapeDtypeStruct((nb, t, d_inner), out_dtype),
                   jax.ShapeDtypeStruct((nb, d_inner, n), F32)],
        scratch_shapes=[pltpu.VMEM((L + TAIL_ROWS, gw), F32), pltpu.VMEM((L + TAIL_ROWS, n), F32),
                        pltpu.VMEM((L + TAIL_ROWS, n), F32), pltpu.VMEM((n, gw), F32)],
        compiler_params=_cparams(("parallel", "parallel", "arbitrary")),
        name="ssd",
    )(*args)
    return y, h


def _lambda_value(lam_ref, lam_init):
    s1 = jnp.sum(lam_ref[0:1, :] * lam_ref[1:2, :], axis=-1, keepdims=True)
    s2 = jnp.sum(lam_ref[2:3, :] * lam_ref[3:4, :], axis=-1, keepdims=True)
    return jnp.exp(s1) - jnp.exp(s2) + lam_init


def _diff_finish(acc1, l1, acc2, l2, lam, g, subw, lam_init):
    o = acc1 / l1 - lam * (acc2 / l2)
    o = o * jax.lax.rsqrt(jnp.mean(o * o, axis=-1, keepdims=True) + NORM_EPS) * subw * (1.0 - lam_init)
    return o * (g * _sigmoid(g))


def _flash_kernel(q_ref, k_ref, v_ref, g_ref, lam_ref, subw_ref, o_ref, m_ref, l_ref, acc_ref, *, dh, lam_init):
    qi = pl.program_id(2)
    ki = pl.program_id(3)
    tq, tk = q_ref.shape[0], k_ref.shape[0]

    @pl.when(ki == 0)
    def _init():
        m_ref[...] = jnp.full(m_ref.shape, -jnp.inf, F32)
        l_ref[...] = jnp.zeros(l_ref.shape, F32)
        acc_ref[...] = jnp.zeros(acc_ref.shape, F32)

    def step(masked):
        vb = v_ref[...].astype(BF16)
        if masked:
            keep = (jax.lax.broadcasted_iota(jnp.int32, (tq, tk), 0)
                    >= jax.lax.broadcasted_iota(jnp.int32, (tq, tk), 1))
        for c in range(2):
            q = q_ref[:, c * dh:(c + 1) * dh]
            k = k_ref[:, c * dh:(c + 1) * dh].astype(BF16)
            s = jax.lax.dot_general(q, k, (((1,), (1,)), ((), ())), preferred_element_type=F32)
            if masked:
                s = jnp.where(keep, s, -jnp.inf)
            m_old = m_ref[c]
            m_new = jnp.maximum(m_old, jnp.max(s, axis=-1, keepdims=True))
            p = jnp.exp(s - m_new)
            corr = jnp.exp(m_old - m_new)
            l_ref[c] = l_ref[c] * corr + jnp.sum(p, axis=-1, keepdims=True)
            acc_ref[c] = acc_ref[c] * corr + jnp.dot(p.astype(BF16), vb, preferred_element_type=F32)
            m_ref[c] = m_new

    @pl.when(ki < qi)
    def _full():
        step(False)

    @pl.when(ki == qi)
    def _diag():
        step(True)
        lam = _lambda_value(lam_ref, lam_init)
        o = _diff_finish(acc_ref[0], l_ref[0], acc_ref[1], l_ref[1], lam, g_ref[...], subw_ref[...], lam_init)
        o_ref[...] = o.astype(o_ref.dtype)


def _flash(q, k, v, g, lam4, subw3, layer, *, nb, t, heads, dh, lam_init, tq=512):
    dv = 2 * dh
    tq = _pick(t, tq)
    nq = t // tq
    qspec = pl.BlockSpec((tq, dv), lambda b, h, i, j: (b * nq + i, h))
    kvspec = pl.BlockSpec((tq, dv), lambda b, h, i, j: (b * nq + jnp.minimum(i, j), h))
    return pl.pallas_call(
        functools.partial(_flash_kernel, dh=dh, lam_init=lam_init),
        grid=(nb, heads, nq, nq),
        in_specs=[qspec, kvspec, kvspec, qspec,
                  pl.BlockSpec((None, 4, dh), lambda b, h, i, j: (layer, 0, 0)),
                  pl.BlockSpec((None, 1, dv), lambda b, h, i, j: (layer, 0, 0))],
        out_specs=qspec,
        out_shape=jax.ShapeDtypeStruct((nb * t, heads * dv), BF16),
        scratch_shapes=[pltpu.VMEM((2, tq, 1), F32), pltpu.VMEM((2, tq, 1), F32), pltpu.VMEM((2, tq, dv), F32)],
        compiler_params=_cparams(("parallel", "parallel", "parallel", "arbitrary")),
        name="flash_diff_attn",
    )(q, k, v, g, lam4, subw3)


def _paged_kernel(pt_ref, q_ref, *refs, n_kt, n_vt, ts, lam_init):
    k_refs, v_refs = refs[:n_kt], refs[n_kt:n_kt + n_vt]
    kn_ref, vn_ref, g_ref, lam_ref, subw_ref, o_ref, m_ref, l_ref, acc_ref = refs[n_kt + n_vt:]
    p = pl.program_id(1)
    n_pages = pl.num_programs(1) - 1
    g_maps = MAP_TILE
    rows_t = g_maps * ts
    kt_per_vt = n_kt // n_vt
    dv = acc_ref.shape[1]

    @pl.when(p == 0)
    def _init():
        m_ref[...] = jnp.full(m_ref.shape, -jnp.inf, F32)
        l_ref[...] = jnp.zeros(l_ref.shape, F32)
        acc_ref[...] = jnp.zeros(acc_ref.shape, F32)

    def block(k_tiles, v_tiles, causal):
        r = k_tiles[0].shape[0]
        row = jax.lax.broadcasted_iota(jnp.int32, (rows_t, r), 0)
        col = jax.lax.broadcasted_iota(jnp.int32, (rows_t, r), 1)
        q_map = row >> (ts.bit_length() - 1)
        if causal:
            q_map = jnp.where((col >> (g_maps.bit_length() - 1)) <= (row & (ts - 1)), q_map, -1)
        same_map = (col & (g_maps - 1)) == q_map
        m_all, l_all, acc_all = m_ref[...], l_ref[...], acc_ref[...]
        m_out, l_out, acc_out = [], [], []
        for ht in range(n_vt):
            ps, corrs = [], []
            for mt in range(ht * kt_per_vt, (ht + 1) * kt_per_vt):
                r0 = mt * rows_t
                s = jax.lax.dot_general(q_ref[r0:r0 + rows_t, :], k_tiles[mt].astype(BF16),
                                        (((1,), (1,)), ((), ())), preferred_element_type=F32)
                s = jnp.where(same_map, s, -jnp.inf)
                m_old = m_all[r0:r0 + rows_t, :]
                m_new = jnp.maximum(m_old, jnp.max(s, axis=-1, keepdims=True))
                pr = jnp.exp(s - m_new)
                corr = jnp.exp(m_old - m_new)
                l_out.append(l_all[r0:r0 + rows_t, :] * corr + jnp.sum(pr, axis=-1, keepdims=True))
                m_out.append(m_new)
                pieces = []
                for m8 in range(g_maps):
                    h8 = ((mt % kt_per_vt) * g_maps + m8) // 2
                    blk = pr[m8 * ts:(m8 + 1) * ts, :]
                    sh = (h8 - m8) % r
                    pieces.append(pltpu.roll(blk, sh, 1) if sh else blk)
                ps.append(jnp.concatenate(pieces, axis=0).astype(BF16))
                corrs.append(corr)
            a0 = ht * kt_per_vt * rows_t
            a1 = a0 + kt_per_vt * rows_t
            pv = jnp.dot(jnp.concatenate(ps, axis=0), v_tiles[ht].astype(BF16), preferred_element_type=F32)
            acc_out.append(acc_all[a0:a1, :] * jnp.concatenate(corrs, axis=0) + pv)
        m_ref[...] = jnp.concatenate(m_out, axis=0)
        l_ref[...] = jnp.concatenate(l_out, axis=0)
        acc_ref[...] = jnp.concatenate(acc_out, axis=0)

    @pl.when(p < n_pages)
    def _cache():
        block([k[...].reshape(-1, k.shape[-1]) for k in k_refs],
              [v[...].reshape(-1, v.shape[-1]) for v in v_refs], False)

    @pl.when(p == n_pages)
    def _new():
        block([kn_ref[i] for i in range(n_kt)], [vn_ref[i] for i in range(n_vt)], True)
        lam = _lambda_value(lam_ref, lam_init)
        for h in range(n_vt * g_maps):
            r1, r2 = 2 * h * ts, (2 * h + 1) * ts
            o = _diff_finish(acc_ref[r1:r1 + ts, :], l_ref[r1:r1 + ts, :], acc_ref[r2:r2 + ts, :],
                             l_ref[r2:r2 + ts, :], lam, g_ref[:, h * dv:(h + 1) * dv], subw_ref[...], lam_init)
            o_ref[:, h * dv:(h + 1) * dv] = o


def _paged(q_t, cache_k, cache_v, k_new, v_new, g, page_table, lam4, subw3, layer, *, lam_init):
    nb, n_pages = page_table.shape
    _, _, page, maps, dh = cache_k.shape
    heads, dv = cache_v.shape[3:]
    ts, width = g.shape[1:]
    n_kt, n_vt = maps // MAP_TILE, heads // MAP_TILE
    assert ts & (ts - 1) == 0 and maps == 2 * heads and n_kt == 2 * n_vt and width == heads * dv
    last = n_pages - 1

    def cache_spec(tile, d):
        return pl.BlockSpec((None, None, page, MAP_TILE, d),
                            lambda b, p, pt: (layer, pt[b * n_pages + jnp.minimum(p, last)], 0, tile, 0))

    def whole(a):
        return pl.BlockSpec((None,) + a.shape[1:], lambda b, p, pt: (b,) + (0,) * (a.ndim - 1))

    in_specs = ([whole(q_t)] + [cache_spec(i, dh) for i in range(n_kt)] + [cache_spec(i, dv) for i in range(n_vt)]
                + [whole(k_new), whole(v_new), whole(g),
                   pl.BlockSpec((None, 4, dh), lambda b, p, pt: (layer, 0, 0)),
                   pl.BlockSpec((None, 1, dv), lambda b, p, pt: (layer, 0, 0))])
    return pl.pallas_call(
        functools.partial(_paged_kernel, n_kt=n_kt, n_vt=n_vt, ts=ts, lam_init=lam_init),
        grid_spec=pltpu.PrefetchScalarGridSpec(
            num_scalar_prefetch=1,
            grid=(nb, n_pages + 1),
            in_specs=in_specs,
            out_specs=pl.BlockSpec((None, ts, width), lambda b, p, pt: (b, 0, 0)),
            scratch_shapes=[pltpu.VMEM((maps * ts, 1), F32), pltpu.VMEM((maps * ts, 1), F32),
                            pltpu.VMEM((maps * ts, dv), F32)]),
        out_shape=jax.ShapeDtypeStruct((nb, ts, width), F32),
        compiler_params=_cparams(("parallel", "arbitrary")),
        name="paged_diff_attn",
    )(page_table.reshape(-1), q_t, *([cache_k] * n_kt), *([cache_v] * n_vt), k_new, v_new, g, lam4, subw3)


def _rope_tables(pos, dh):
    half = dh // 2
    inv = ROPE_THETA ** (-jnp.arange(half, dtype=F32) / half)
    ang = pos.astype(F32)[:, None] * inv[None]
    cos, sin = jnp.cos(ang), jnp.sin(ang)
    return jnp.concatenate([cos, cos], -1), jnp.concatenate([-sin, sin], -1)


def kernel(x_prompt, x_sample, state_conv, state_ssm, cache_k, cache_v, page_table, ssm_w_in, ssm_conv_w, ssm_conv_b, ssm_dt_bias, ssm_a_log, ssm_d, ssm_norm_w, ssm_w_out, attn_w_in, attn_lambda_q1, attn_lambda_k1, attn_lambda_q2, attn_lambda_k2, attn_subln_w, attn_w_out, ln_g, ln_b):
    bp, tp, d_model = x_prompt.shape
    bs, ts, _ = x_sample.shape
    depth = ln_g.shape[0]
    n_mixers = 2
    alpha = (2 * depth) ** 0.25
    past_len = page_table.shape[1] * cache_k.shape[2]

    _, _, nh, hp, n_state = state_ssm.shape
    d_inner = nh * hp
    conv_dim = state_conv.shape[-1]
    zx_w = d_inner + conv_dim
    n_pool, page, kheads, dh = cache_k.shape[1:]
    heads = kheads // 2
    dv = 2 * dh
    qk_dim = kheads * dh
    width = heads * dv
    assert n_state == LANES and dh == LANES and 2 * hp == LANES and qk_dim == width

    ln_g3, ln_b3 = ln_g.reshape(depth, 1, d_model), ln_b.reshape(depth, 1, d_model)
    ssm_prm = (ssm_conv_w, ssm_conv_b.reshape(-1, 1, conv_dim), ssm_dt_bias.reshape(-1, 1, nh),
               ssm_a_log.reshape(-1, 1, nh), ssm_d.reshape(-1, 1, nh), ssm_norm_w.reshape(-1, 1, d_inner))
    lam4 = jnp.stack([attn_lambda_q1, attn_lambda_k1, attn_lambda_q2, attn_lambda_k2], axis=1)
    subw3 = attn_subln_w.reshape(-1, 1, dv)
    state_ssm4 = state_ssm.reshape(state_ssm.shape[0], bs, d_inner, n_state)

    cos_p, sin_p = _rope_tables(jnp.tile(jnp.arange(tp), bp), dh)
    cos_s, sin_s = _rope_tables(jnp.tile(past_len + jnp.arange(ts), bs), dh)

    yp = x_prompt.reshape(bp * tp, d_model)
    ys = x_sample.reshape(bs * ts, d_model)
    ypb, ysb = yp.astype(BF16), ys.astype(BF16)
    conv_p, ssm_p, k_p, v_p, conv_s, ssm_s, k_s, v_s = ([] for _ in range(8))

    def ssm_layer(xb, j, nb, t, conv_buf, h0):
        zx = _matmul(xb, ssm_w_in, j, n=zx_w, name="ssm_in_proj").reshape(nb, t, zx_w)
        dt_raw = _matmul(xb, ssm_w_in, j, n_off=zx_w, n=nh, name="ssm_dt_proj").reshape(nb, t, nh)
        y, h = _ssd(zx, dt_raw, ssm_prm, j, out_dtype=BF16 if t >= SSM_CHUNK else F32, conv_buf=conv_buf, h0=h0)
        mix = _matmul(y.reshape(nb * t, d_inner), ssm_w_out, j, name="ssm_out_proj")
        assert t >= CONV_WIDTH - 1
        return mix, zx[:, t - (CONV_WIDTH - 1):, d_inner:], h.reshape(nb, nh, hp, n_state)

    def attn_project(xb, j, cos, sin):
        q = _matmul(xb, attn_w_in, j, n=qk_dim, out_dtype=BF16, rope=(cos, sin, dh ** -0.5), name="attn_q_proj")
        k = _matmul(xb, attn_w_in, j, n_off=qk_dim, n=qk_dim, rope=(cos, sin, 1.0), name="attn_k_proj")
        v = _matmul(xb, attn_w_in, j, n_off=2 * qk_dim, n=width, name="attn_v_proj")
        g = _matmul(xb, attn_w_in, j, n_off=2 * qk_dim + width, n=width, name="attn_g_proj")
        return q, k, v, g

    for i in range(depth):
        j = i // n_mixers
        if i % n_mixers == 0:
            mp, cb, h = ssm_layer(ypb, j, bp, tp, None, None)
            conv_p.append(cb); ssm_p.append(h)
            ms, cb, h = ssm_layer(ysb, j, bs, ts, state_conv[j], state_ssm4[j])
            conv_s.append(cb); ssm_s.append(h)
        else:
            lam_init = 0.8 - 0.6 * math.exp(-0.3 * i)
            q, k, v, g = attn_project(ypb, j, cos_p, sin_p)
            og = _flash(q, k, v, g, lam4, subw3, j, nb=bp, t=tp, heads=heads, dh=dh, lam_init=lam_init)
            mp = _matmul(og, attn_w_out, j, name="attn_out_proj")
            k_p.append(k.reshape(bp, tp, kheads, dh)); v_p.append(v.reshape(bp, tp, heads, dv))

            q, k, v, g = attn_project(ysb, j, cos_s, sin_s)
            q_t = q.reshape(bs, ts, kheads, dh).transpose(0, 2, 1, 3).reshape(bs, kheads * ts, dh)

            def tiled(a, n, d):
                a = a.reshape(bs, ts, n // MAP_TILE, MAP_TILE, d).transpose(0, 2, 1, 3, 4)
                a = a.reshape(bs, n // MAP_TILE, ts * MAP_TILE, d)
                return jnp.pad(a, ((0, 0), (0, 0), (0, -(ts * MAP_TILE) % LANES), (0, 0)))

            og = _paged(q_t, cache_k, cache_v, tiled(k, kheads, dh), tiled(v, heads, dv),
                        g.reshape(bs, ts, width), page_table, lam4, subw3, j, lam_init=lam_init)
            ms = _matmul(og.reshape(bs * ts, width), attn_w_out, j, name="attn_out_proj")
            k_s.append(k.reshape(bs, ts, kheads, dh)); v_s.append(v.reshape(bs, ts, heads, dv))
        yp, ypb = _add_ln(yp, mp, ln_g3, ln_b3, i, alpha)
        ys, ysb = _add_ln(ys, ms, ln_g3, ln_b3, i, alpha)

    return (yp.reshape(bp, tp, d_model), ys.reshape(bs, ts, d_model),
            jnp.stack(conv_p), jnp.stack(ssm_p), jnp.stack(k_p), jnp.stack(v_p),
            jnp.stack(conv_s), jnp.stack(ssm_s), jnp.stack(k_s), jnp.stack(v_s))
```

```python
import functools
import math

import jax
import jax.numpy as jnp
from jax.experimental import pallas as pl
from jax.experimental.pallas import tpu as pltpu

F32 = jnp.float32
BF16 = jnp.bfloat16

NORM_EPS = 1e-5
ROPE_THETA = 10000.0
CONV_WIDTH = 4
LOG2E = 1.4426950408889634
SSM_CHUNK = 128
TAIL_ROWS = 8
LANES = 128
MAP_TILE = 8
VMEM_LIMIT = 56 * 1024 * 1024


def _cparams(sem):
    return pltpu.CompilerParams(dimension_semantics=sem, vmem_limit_bytes=VMEM_LIMIT)


def _sigmoid(x):
    return 1.0 / (1.0 + jnp.exp(-x))


def _pick(n, pref):
    if n <= pref:
        return n
    t = pref
    while n % t:
        t //= 2
    return t


def _mm_kernel(x_ref, w_ref, o_ref):
    o_ref[...] = jnp.dot(x_ref[...].astype(BF16), w_ref[...].astype(BF16),
                         preferred_element_type=F32).astype(o_ref.dtype)


def _mm_t_kernel(x_ref, w_ref, o_ref, ot_ref):
    y = jnp.dot(x_ref[...].astype(BF16), w_ref[...].astype(BF16), preferred_element_type=F32)
    o_ref[...] = y.astype(o_ref.dtype)
    ot_ref[...] = y.T.astype(ot_ref.dtype)


def _mm_rope_kernel(x_ref, w_ref, cos_ref, sin_ref, o_ref, *, scale):
    y = jnp.dot(x_ref[...].astype(BF16), w_ref[...].astype(BF16), preferred_element_type=F32)
    cos = cos_ref[...]
    sin = sin_ref[...]
    for h in range(y.shape[1] // LANES):
        yh = y[:, h * LANES:(h + 1) * LANES]
        r = yh * cos + pltpu.roll(yh, LANES // 2, 1) * sin
        if scale != 1.0:
            r = r * scale
        o_ref[:, h * LANES:(h + 1) * LANES] = r.astype(o_ref.dtype)


RESIDENT_PANEL_BYTES = 16 * 1024 * 1024


def _mm_tiles(m, kdim, n, x_bytes, rope):
    if m <= 256:
        return m, _pick(n, 2 * RESIDENT_PANEL_BYTES // (kdim * 4 * 2)), 2
    tm = _pick(m, 2048)
    if tm * kdim * x_bytes > RESIDENT_PANEL_BYTES:
        return tm // 2, _pick(n, 256), 1
    return (tm // 2 if rope else tm), _pick(n, 256), 2


def _matmul(x, w3, layer, *, n_off=0, n=None, out_dtype=F32, rope=None, transposed_copy=False, name="matmul"):
    m, kdim = x.shape
    n = w3.shape[2] - n_off if n is None else n
    tm, tn, x_buffers = _mm_tiles(m, kdim, n, x.dtype.itemsize, rope is not None or transposed_copy)
    assert n_off % tn == 0
    off = n_off // tn
    x_mode = {} if x_buffers == 2 else {"pipeline_mode": pl.Buffered(x_buffers)}
    in_specs = [pl.BlockSpec((tm, kdim), lambda i, j: (i, 0), **x_mode),
                pl.BlockSpec((None, kdim, tn), lambda i, j: (layer, 0, j + off))]
    args = [x, w3]
    body = _mm_kernel
    if rope is not None:
        cos, sin, scale = rope
        tab = pl.BlockSpec((tm, LANES), lambda i, j: (i, 0))
        in_specs += [tab, tab]
        args += [cos, sin]
        body = functools.partial(_mm_rope_kernel, scale=scale)
    out_specs = pl.BlockSpec((tm, tn), lambda i, j: (i, j))
    out_shape = jax.ShapeDtypeStruct((m, n), out_dtype)
    if transposed_copy:
        body = _mm_t_kernel
        out_specs = [out_specs, pl.BlockSpec((tn, tm), lambda i, j: (j, i))]
        out_shape = [out_shape, jax.ShapeDtypeStruct((n, m), BF16)]
    return pl.pallas_call(
        body,
        grid=(m // tm, n // tn),
        in_specs=in_specs,
        out_specs=out_specs,
        out_shape=out_shape,
        compiler_params=_cparams(("parallel", "parallel")),
        name=name,
    )(*args)


def _add_ln_kernel(x_ref, m_ref, g_ref, b_ref, o_ref, ob_ref, *, alpha):
    v = alpha * x_ref[...] + m_ref[...]
    mu = jnp.mean(v, axis=-1, keepdims=True)
    d = v - mu
    var = jnp.mean(d * d, axis=-1, keepdims=True)
    y = d * jax.lax.rsqrt(var + NORM_EPS) * g_ref[...] + b_ref[...]
    o_ref[...] = y
    ob_ref[...] = y.astype(BF16)


def _add_ln(x, mix, g3, b3, layer, alpha, *, tm=256):
    m, d = x.shape
    tm = _pick(m, tm)
    row = pl.BlockSpec((tm, d), lambda i: (i, 0))
    par = pl.BlockSpec((None, 1, d), lambda i: (layer, 0, 0))
    return pl.pallas_call(
        functools.partial(_add_ln_kernel, alpha=alpha),
        grid=(m // tm,),
        in_specs=[row, row, par, par],
        out_specs=[row, row],
        out_shape=[jax.ShapeDtypeStruct((m, d), F32), jax.ShapeDtypeStruct((m, d), BF16)],
        compiler_params=_cparams(("parallel",)),
        name="add_ln",
    )(x, mix, g3, b3)


def _ssd_kernel(*refs, valid, hg, zero_init):
    if zero_init:
        (z_ref, x_ref, b_ref, c_ref, dt_ref, cwx_ref, cwb_ref, cwc_ref, cbx_ref, cbb_ref, cbc_ref,
         dtb_ref, alog_ref, dsk_ref, nw_ref, y_ref, h_ref, ext_x, ext_b, ext_c, h_t) = refs
    else:
        (z_ref, x_ref, b_ref, c_ref, dt_ref, cwx_ref, cwb_ref, cwc_ref, cbx_ref, cbb_ref, cbc_ref,
         dtb_ref, alog_ref, dsk_ref, nw_ref, bufx_ref, bufb_ref, bufc_ref, h0_ref,
         y_ref, h_ref, ext_x, ext_b, ext_c, h_t) = refs
    L = SSM_CHUNK
    g = pl.program_id(1)
    c = pl.program_id(2)
    nc = pl.num_programs(2)
    exts = (ext_x, ext_b, ext_c)

    @pl.when(c == 0)
    def _init():
        if zero_init:
            for e in exts:
                e[0:TAIL_ROWS, :] = jnp.zeros((TAIL_ROWS, e.shape[1]), F32)
            h_t[...] = jnp.zeros(h_t.shape, F32)
        else:
            for e, buf in zip(exts, (bufx_ref, bufb_ref, bufc_ref)):
                e[...] = jnp.zeros(e.shape, F32)
                e[TAIL_ROWS - (CONV_WIDTH - 1):TAIL_ROWS, :] = buf[...]
            h_t[...] = h0_ref[...].T

    for e, raw in zip(exts, (x_ref, b_ref, c_ref)):
        e[TAIL_ROWS:TAIL_ROWS + valid, :] = raw[...]

    def conv_silu(e, cw_ref, cb_ref):
        acc = cb_ref[...] + cw_ref[CONV_WIDTH - 1:CONV_WIDTH, :] * e[TAIL_ROWS:TAIL_ROWS + L, :]
        for s in range(1, CONV_WIDTH):
            acc = acc + cw_ref[CONV_WIDTH - 1 - s:CONV_WIDTH - s, :] * e[TAIL_ROWS - s:TAIL_ROWS - s + L, :]
        return acc * _sigmoid(acc)

    xs = conv_silu(ext_x, cwx_ref, cbx_ref)
    bm = conv_silu(ext_b, cwb_ref, cbb_ref)
    cm = conv_silu(ext_c, cwc_ref, cbc_ref)

    if valid == L:
        for e in exts:
            e[0:TAIL_ROWS, :] = e[L:L + TAIL_ROWS, :]

    dtv = dt_ref[...] + dtb_ref[...]
    dt = jnp.maximum(dtv, 0.0) + jnp.log1p(jnp.exp(-jnp.abs(dtv)))
    if valid < L:
        dt = jnp.concatenate([dt, jnp.zeros((L - valid, dt.shape[1]), F32)], axis=0)
    da = dt * (-jnp.exp(alog_ref[...]) * LOG2E)
    rows = jax.lax.broadcasted_iota(jnp.int32, (L, L), 0)
    cols = jax.lax.broadcasted_iota(jnp.int32, (L, L), 1)
    causal = rows >= cols
    tril = jnp.where(causal, 1.0, 0.0).astype(BF16)
    p0 = da.astype(BF16)
    r1 = da - p0.astype(F32)
    p1 = r1.astype(BF16)
    p2 = (r1 - p1.astype(F32)).astype(BF16)
    a_cs = (jnp.dot(tril, p0, preferred_element_type=F32) + jnp.dot(tril, p1, preferred_element_type=F32)
            + jnp.dot(tril, p2, preferred_element_type=F32))

    nh = dt.shape[1]
    shift = jax.lax.rem(nh - g * hg, nh)
    a_g = pltpu.roll(a_cs, shift, 1)
    dt_g = pltpu.roll(dt, shift, 1)
    dsk_g = pltpu.roll(dsk_ref[...], shift, 1)
    a_gt = a_g.T

    cmb = cm.astype(BF16)
    cb = jax.lax.dot_general(cmb, bm.astype(BF16), (((1,), (1,)), ((), ())), preferred_element_type=F32)
    y_off = jnp.dot(cmb, h_t[...].astype(BF16), preferred_element_type=F32)

    lane = jax.lax.broadcasted_iota(jnp.int32, (L, LANES), 1)
    lo_half = lane < (LANES // 2)
    lo_half1 = lo_half[0:1, :]
    ys, xdds, cds = [], [], []
    for q in range(hg // 2):
        j0, j1 = 2 * q, 2 * q + 1
        col0 = jnp.broadcast_to(a_g[:, j0:j0 + 1], (L, LANES))
        col1 = jnp.broadcast_to(a_g[:, j1:j1 + 1], (L, LANES))
        m0 = (cb * jnp.exp2(jnp.where(causal, col0 - a_gt[j0:j0 + 1, :], -jnp.inf))).astype(BF16)
        m1 = (cb * jnp.exp2(jnp.where(causal, col1 - a_gt[j1:j1 + 1, :], -jnp.inf))).astype(BF16)
        dtp = jnp.where(lo_half, jnp.broadcast_to(dt_g[:, j0:j0 + 1], (L, LANES)),
                        jnp.broadcast_to(dt_g[:, j1:j1 + 1], (L, LANES)))
        colp = jnp.where(lo_half, col0, col1)
        dskp = jnp.where(lo_half1, jnp.broadcast_to(dsk_g[:, j0:j0 + 1], (1, LANES)),
                         jnp.broadcast_to(dsk_g[:, j1:j1 + 1], (1, LANES)))
        xs_p = xs[:, q * LANES:(q + 1) * LANES]
        xd_p = xs_p * dtp
        xd_b = xd_p.astype(BF16)
        y_diag = jnp.where(lo_half, jnp.dot(m0, xd_b, preferred_element_type=F32),
                           jnp.dot(m1, xd_b, preferred_element_type=F32))
        e_p = jnp.exp2(colp)
        ys.append(y_diag + y_off[:, q * LANES:(q + 1) * LANES] * e_p + dskp * xs_p)
        xdds.append((xd_p * jnp.exp2(colp[L - 1:L, :] - colp)).astype(BF16))
        cds.append(e_p[L - 1:L, :])
    y = jnp.concatenate(ys, axis=1)[0:valid, :]
    xdd = jnp.concatenate(xdds, axis=1)
    chunk_decay = jnp.concatenate(cds, axis=1)

    s_t = jnp.dot(bm.T.astype(BF16), xdd, preferred_element_type=F32)
    h_t[...] = h_t[...] * chunk_decay + s_t

    zg = z_ref[...]
    y = y * (zg * _sigmoid(zg))
    y = y * jax.lax.rsqrt(jnp.mean(y * y, axis=-1, keepdims=True) + NORM_EPS) * nw_ref[...]
    y_ref[...] = y.astype(y_ref.dtype)

    @pl.when(c == nc - 1)
    def _fin():
        h_ref[...] = h_t[...].T


def _ssd(zx, dt_raw, prm, layer, *, out_dtype, conv_buf=None, h0=None):
    conv_w, conv_b, dt_bias, a_log, d_skip, norm_w = prm
    nb, t, nh = dt_raw.shape
    d_inner = norm_w.shape[2]
    conv_dim = conv_w.shape[2]
    gn = (conv_dim - d_inner) // 2
    n = LANES
    groups = gn // n
    hg = nh // groups
    gw = d_inner // groups
    zero_init = conv_buf is None
    L = SSM_CHUNK
    if t >= L:
        assert t % L == 0
        valid, nc = L, t // L
    else:
        assert t % TAIL_ROWS == 0
        valid, nc = t, 1
    xo, bo, co = d_inner // gw, 2 * d_inner // n, (2 * d_inner + gn) // n

    def rowblk(width, off):
        return pl.BlockSpec((None, valid, width), lambda b, g, c: (b, c, off + g))

    def par3(rows, width, off):
        return pl.BlockSpec((None, rows, width), lambda b, g, c: (layer, 0, off + g))

    full = pl.BlockSpec((None, 1, nh), lambda b, g, c: (layer, 0, 0))
    in_specs = [rowblk(gw, 0), rowblk(gw, xo), rowblk(n, bo), rowblk(n, co),
                pl.BlockSpec((None, valid, nh), lambda b, g, c: (b, c, 0)),
                par3(CONV_WIDTH, gw, 0), par3(CONV_WIDTH, n, d_inner // n), par3(CONV_WIDTH, n, (d_inner + gn) // n),
                par3(1, gw, 0), par3(1, n, d_inner // n), par3(1, n, (d_inner + gn) // n),
                full, full, full, par3(1, gw, 0)]
    args = [zx, zx, zx, zx, dt_raw, conv_w, conv_w, conv_w, conv_b, conv_b, conv_b, dt_bias, a_log, d_skip, norm_w]
    if not zero_init:
        def bufblk(width, off):
            return pl.BlockSpec((None, CONV_WIDTH - 1, width), lambda b, g, c: (b, 0, off + g))
        in_specs += [bufblk(gw, 0), bufblk(n, d_inner // n), bufblk(n, (d_inner + gn) // n),
                     pl.BlockSpec((None, gw, n), lambda b, g, c: (b, g, 0))]
        args += [conv_buf, conv_buf, conv_buf, h0]
    y, h = pl.pallas_call(
        functools.partial(_ssd_kernel, valid=valid, hg=hg, zero_init=zero_init),
        grid=(nb, groups, nc),
        in_specs=in_specs,
        out_specs=[rowblk(gw, 0),
                   pl.BlockSpec((None, gw, n), lambda b, g, c: (b, g, 0))],
        out_shape=[jax.ShapeDtypeStruct((nb, t, d_inner), out_dtype),
                   jax.ShapeDtypeStruct((nb, d_inner, n), F32)],
        scratch_shapes=[pltpu.VMEM((L + TAIL_ROWS, gw), F32), pltpu.VMEM((L + TAIL_ROWS, n), F32),
                        pltpu.VMEM((L + TAIL_ROWS, n), F32), pltpu.VMEM((n, gw), F32)],
        compiler_params=_cparams(("parallel", "parallel", "arbitrary")),
        name="ssd",
    )(*args)
    return y, h


def _lambda_value(lam_ref, lam_init):
    s1 = jnp.sum(lam_ref[0:1, :] * lam_ref[1:2, :], axis=-1, keepdims=True)
    s2 = jnp.sum(lam_ref[2:3, :] * lam_ref[3:4, :], axis=-1, keepdims=True)
    return jnp.exp(s1) - jnp.exp(s2) + lam_init


def _norm_gate(o, g, subw, lam_init):
    o = o * jax.lax.rsqrt(jnp.mean(o * o, axis=-1, keepdims=True) + NORM_EPS) * subw * (1.0 - lam_init)
    return o * (g * _sigmoid(g))


def _diff_finish(acc1, l1, acc2, l2, lam, g, subw, lam_init):
    return _norm_gate(acc1 / l1 - lam * (acc2 / l2), g, subw, lam_init)


def _flash_kernel(q_ref, k_ref, vt_ref, g_ref, lam_ref, subw_ref, o_ref, m_ref, l_ref, acc_ref, *, dh, lam_init):
    qi = pl.program_id(2)
    ki = pl.program_id(3)
    tq, tk = q_ref.shape[0], k_ref.shape[0]

    @pl.when(ki == 0)
    def _init():
        m_ref[...] = jnp.full(m_ref.shape, -jnp.inf, F32)
        l_ref[...] = jnp.zeros(l_ref.shape, F32)
        acc_ref[...] = jnp.zeros(acc_ref.shape, F32)

    def step(masked):
        vt = vt_ref[...]
        if masked:
            keep = (jax.lax.broadcasted_iota(jnp.int32, (tk, tq), 0)
                    <= jax.lax.broadcasted_iota(jnp.int32, (tk, tq), 1))
        for c in range(2):
            st = jax.lax.dot_general(k_ref[:, c * dh:(c + 1) * dh].astype(BF16), q_ref[:, c * dh:(c + 1) * dh],
                                     (((1,), (1,)), ((), ())), preferred_element_type=F32)
            if masked:
                st = jnp.where(keep, st, -jnp.inf)
            m_old = m_ref[c]
            m_new = jnp.maximum(m_old, jnp.max(st, axis=0, keepdims=True))
            pt = jnp.exp2(st - m_new)
            corr = jnp.exp2(m_old - m_new)
            l_ref[c] = l_ref[c] * corr + jnp.sum(pt, axis=0, keepdims=True)
            acc_ref[c] = acc_ref[c] * corr + jnp.dot(vt, pt.astype(BF16), preferred_element_type=F32)
            m_ref[c] = m_new

    @pl.when(ki < qi)
    def _full():
        step(False)

    @pl.when(ki == qi)
    def _diag():
        step(True)
        lam = _lambda_value(lam_ref, lam_init)
        dt = acc_ref[0] / l_ref[0] - lam * (acc_ref[1] / l_ref[1])
        o = _norm_gate(dt.T, g_ref[...], subw_ref[...], lam_init)
        o_ref[...] = o.astype(o_ref.dtype)


def _flash(q, k, vt, g, lam4, subw3, layer, *, nb, t, heads, dh, lam_init, tq=512):
    dv = 2 * dh
    tq = _pick(t, tq)
    nq = t // tq
    qspec = pl.BlockSpec((tq, dv), lambda b, h, i, j: (b * nq + i, h))
    return pl.pallas_call(
        functools.partial(_flash_kernel, dh=dh, lam_init=lam_init),
        grid=(nb, heads, nq, nq),
        in_specs=[qspec,
                  pl.BlockSpec((tq, dv), lambda b, h, i, j: (b * nq + jnp.minimum(i, j), h)),
                  pl.BlockSpec((dv, tq), lambda b, h, i, j: (h, b * nq + jnp.minimum(i, j))),
                  qspec,
                  pl.BlockSpec((None, 4, dh), lambda b, h, i, j: (layer, 0, 0)),
                  pl.BlockSpec((None, 1, dv), lambda b, h, i, j: (layer, 0, 0))],
        out_specs=qspec,
        out_shape=jax.ShapeDtypeStruct((nb * t, heads * dv), BF16),
        scratch_shapes=[pltpu.VMEM((2, 1, tq), F32), pltpu.VMEM((2, 1, tq), F32), pltpu.VMEM((2, dv, tq), F32)],
        compiler_params=_cparams(("parallel", "parallel", "parallel", "arbitrary")),
        name="flash_diff_attn",
    )(q, k, vt, g, lam4, subw3)


def _paged_kernel(pt_ref, q_ref, *refs, n_kt, n_vt, ts, lam_init):
    k_refs, v_refs = refs[:n_kt], refs[n_kt:n_kt + n_vt]
    kn_ref, vn_ref, g_ref, lam_ref, subw_ref, o_ref, m_ref, l_ref, acc_ref = refs[n_kt + n_vt:]
    p = pl.program_id(1)
    n_pages = pl.num_programs(1) - 1
    g_maps = MAP_TILE
    rows_t = g_maps * ts
    kt_per_vt = n_kt // n_vt
    dv = acc_ref.shape[1]

    @pl.when(p == 0)
    def _init():
        m_ref[...] = jnp.full(m_ref.shape, -jnp.inf, F32)
        l_ref[...] = jnp.zeros(l_ref.shape, F32)
        acc_ref[...] = jnp.zeros(acc_ref.shape, F32)

    def block(k_tiles, v_tiles, causal):
        r = k_tiles[0].shape[0]
        row = jax.lax.broadcasted_iota(jnp.int32, (rows_t, r), 0)
        col = jax.lax.broadcasted_iota(jnp.int32, (rows_t, r), 1)
        q_map = row >> (ts.bit_length() - 1)
        if causal:
            q_map = jnp.where((col >> (g_maps.bit_length() - 1)) <= (row & (ts - 1)), q_map, -1)
        same_map = (col & (g_maps - 1)) == q_map
        ss = [jax.lax.dot_general(q_ref[mt * rows_t:(mt + 1) * rows_t, :], k_tiles[mt].astype(BF16),
                                  (((1,), (1,)), ((), ())), preferred_element_type=F32) for mt in range(n_kt)]
        s = jnp.where(jnp.concatenate([same_map] * n_kt, axis=0), jnp.concatenate(ss, axis=0), -jnp.inf)
        m_old = m_ref[...]
        m_new = jnp.maximum(m_old, jnp.max(s, axis=-1, keepdims=True))
        pr = jnp.exp2(s - m_new)
        corr = jnp.exp2(m_old - m_new)
        l_ref[...] = l_ref[...] * corr + jnp.sum(pr, axis=-1, keepdims=True)
        m_ref[...] = m_new
        pieces = []
        for m in range(n_kt * g_maps):
            h8 = (m % (kt_per_vt * g_maps)) // 2
            blk = pr[m * ts:(m + 1) * ts, :]
            sh = (h8 - m % g_maps) % r
            pieces.append(pltpu.roll(blk, sh, 1) if sh else blk)
        per_vt = kt_per_vt * g_maps
        pv = [jnp.dot(jnp.concatenate(pieces[ht * per_vt:(ht + 1) * per_vt], axis=0).astype(BF16),
                      v_tiles[ht].astype(BF16), preferred_element_type=F32) for ht in range(n_vt)]
        acc_ref[...] = acc_ref[...] * corr + jnp.concatenate(pv, axis=0)

    @pl.when(p < n_pages)
    def _cache():
        block([k[...].reshape(-1, k.shape[-1]) for k in k_refs],
              [v[...].reshape(-1, v.shape[-1]) for v in v_refs], False)

    @pl.when(p == n_pages)
    def _new():
        block([kn_ref[i] for i in range(n_kt)], [vn_ref[i] for i in range(n_vt)], True)
        lam = _lambda_value(lam_ref, lam_init)
        for h in range(n_vt * g_maps):
            r1, r2 = 2 * h * ts, (2 * h + 1) * ts
            o = _diff_finish(acc_ref[r1:r1 + ts, :], l_ref[r1:r1 + ts, :], acc_ref[r2:r2 + ts, :],
                             l_ref[r2:r2 + ts, :], lam, g_ref[:, h * dv:(h + 1) * dv], subw_ref[...], lam_init)
            o_ref[:, h * dv:(h + 1) * dv] = o


def _paged(q_t, cache_k, cache_v, k_new, v_new, g, page_table, lam4, subw3, layer, *, lam_init):
    nb, n_pages = page_table.shape
    _, _, page, maps, dh = cache_k.shape
    heads, dv = cache_v.shape[3:]
    ts, width = g.shape[1:]
    n_kt, n_vt = maps // MAP_TILE, heads // MAP_TILE
    assert ts & (ts - 1) == 0 and maps == 2 * heads and n_kt == 2 * n_vt and width == heads * dv
    last = n_pages - 1

    def cache_spec(tile, d):
        return pl.BlockSpec((None, None, page, MAP_TILE, d),
                            lambda b, p, pt: (layer, pt[b * n_pages + jnp.minimum(p, last)], 0, tile, 0))

    def whole(a):
        return pl.BlockSpec((None,) + a.shape[1:], lambda b, p, pt: (b,) + (0,) * (a.ndim - 1))

    in_specs = ([whole(q_t)] + [cache_spec(i, dh) for i in range(n_kt)] + [cache_spec(i, dv) for i in range(n_vt)]
                + [whole(k_new), whole(v_new), whole(g),
                   pl.BlockSpec((None, 4, dh), lambda b, p, pt: (layer, 0, 0)),
                   pl.BlockSpec((None, 1, dv), lambda b, p, pt: (layer, 0, 0))])
    return pl.pallas_call(
        functools.partial(_paged_kernel, n_kt=n_kt, n_vt=n_vt, ts=ts, lam_init=lam_init),
        grid_spec=pltpu.PrefetchScalarGridSpec(
            num_scalar_prefetch=1,
            grid=(nb, n_pages + 1),
            in_specs=in_specs,
            out_specs=pl.BlockSpec((None, ts, width), lambda b, p, pt: (b, 0, 0)),
            scratch_shapes=[pltpu.VMEM((maps * ts, 1), F32), pltpu.VMEM((maps * ts, 1), F32),
                            pltpu.VMEM((maps * ts, dv), F32)]),
        out_shape=jax.ShapeDtypeStruct((nb, ts, width), F32),
        compiler_params=_cparams(("parallel", "arbitrary")),
        name="paged_diff_attn",
    )(page_table.reshape(-1), q_t, *([cache_k] * n_kt), *([cache_v] * n_vt), k_new, v_new, g, lam4, subw3)


def _rope_tables(pos, dh):
    half = dh // 2
    inv = ROPE_THETA ** (-jnp.arange(half, dtype=F32) / half)
    ang = pos.astype(F32)[:, None] * inv[None]
    cos, sin = jnp.cos(ang), jnp.sin(ang)
    return jnp.concatenate([cos, cos], -1), jnp.concatenate([-sin, sin], -1)


def kernel(x_prompt, x_sample, state_conv, state_ssm, cache_k, cache_v, page_table, ssm_w_in, ssm_conv_w, ssm_conv_b, ssm_dt_bias, ssm_a_log, ssm_d, ssm_norm_w, ssm_w_out, attn_w_in, attn_lambda_q1, attn_lambda_k1, attn_lambda_q2, attn_lambda_k2, attn_subln_w, attn_w_out, ln_g, ln_b):
    bp, tp, d_model = x_prompt.shape
    bs, ts, _ = x_sample.shape
    depth = ln_g.shape[0]
    n_mixers = 2
    alpha = (2 * depth) ** 0.25
    past_len = page_table.shape[1] * cache_k.shape[2]

    _, _, nh, hp, n_state = state_ssm.shape
    d_inner = nh * hp
    conv_dim = state_conv.shape[-1]
    zx_w = d_inner + conv_dim
    n_pool, page, kheads, dh = cache_k.shape[1:]
    heads = kheads // 2
    dv = 2 * dh
    qk_dim = kheads * dh
    width = heads * dv
    assert n_state == LANES and dh == LANES and 2 * hp == LANES and qk_dim == width

    ln_g3, ln_b3 = ln_g.reshape(depth, 1, d_model), ln_b.reshape(depth, 1, d_model)
    ssm_prm = (ssm_conv_w, ssm_conv_b.reshape(-1, 1, conv_dim), ssm_dt_bias.reshape(-1, 1, nh),
               ssm_a_log.reshape(-1, 1, nh), ssm_d.reshape(-1, 1, nh), ssm_norm_w.reshape(-1, 1, d_inner))
    lam4 = jnp.stack([attn_lambda_q1, attn_lambda_k1, attn_lambda_q2, attn_lambda_k2], axis=1)
    subw3 = attn_subln_w.reshape(-1, 1, dv)
    state_ssm4 = state_ssm.reshape(state_ssm.shape[0], bs, d_inner, n_state)

    cos_p, sin_p = _rope_tables(jnp.tile(jnp.arange(tp), bp), dh)
    cos_s, sin_s = _rope_tables(jnp.tile(past_len + jnp.arange(ts), bs), dh)

    yp = x_prompt.reshape(bp * tp, d_model)
    ys = x_sample.reshape(bs * ts, d_model)
    ypb, ysb = yp.astype(BF16), ys.astype(BF16)
    conv_p, ssm_p, k_p, v_p, conv_s, ssm_s, k_s, v_s = ([] for _ in range(8))

    def ssm_layer(xb, j, nb, t, conv_buf, h0):
        zx = _matmul(xb, ssm_w_in, j, n=zx_w, name="ssm_in_proj").reshape(nb, t, zx_w)
        dt_raw = _matmul(xb, ssm_w_in, j, n_off=zx_w, n=nh, name="ssm_dt_proj").reshape(nb, t, nh)
        y, h = _ssd(zx, dt_raw, ssm_prm, j, out_dtype=BF16 if t >= SSM_CHUNK else F32, conv_buf=conv_buf, h0=h0)
        mix = _matmul(y.reshape(nb * t, d_inner), ssm_w_out, j, name="ssm_out_proj")
        assert t >= CONV_WIDTH - 1
        return mix, zx[:, t - (CONV_WIDTH - 1):, d_inner:], h.reshape(nb, nh, hp, n_state)

    def attn_project(xb, j, cos, sin, v_transposed):
        q = _matmul(xb, attn_w_in, j, n=qk_dim, out_dtype=BF16, rope=(cos, sin, dh ** -0.5 * LOG2E),
                    name="attn_q_proj")
        k = _matmul(xb, attn_w_in, j, n_off=qk_dim, n=qk_dim, rope=(cos, sin, 1.0), name="attn_k_proj")
        v = _matmul(xb, attn_w_in, j, n_off=2 * qk_dim, n=width, transposed_copy=v_transposed, name="attn_v_proj")
        g = _matmul(xb, attn_w_in, j, n_off=2 * qk_dim + width, n=width, name="attn_g_proj")
        return q, k, v, g

    for i in range(depth):
        j = i // n_mixers
        if i % n_mixers == 0:
            mp, cb, h = ssm_layer(ypb, j, bp, tp, None, None)
            conv_p.append(cb); ssm_p.append(h)
            ms, cb, h = ssm_layer(ysb, j, bs, ts, state_conv[j], state_ssm4[j])
            conv_s.append(cb); ssm_s.append(h)
        else:
            lam_init = 0.8 - 0.6 * math.exp(-0.3 * i)
            q, k, (v, vt), g = attn_project(ypb, j, cos_p, sin_p, True)
            og = _flash(q, k, vt, g, lam4, subw3, j, nb=bp, t=tp, heads=heads, dh=dh, lam_init=lam_init)
            mp = _matmul(og, attn_w_out, j, name="attn_out_proj")
            k_p.append(k.reshape(bp, tp, kheads, dh)); v_p.append(v.reshape(bp, tp, heads, dv))

            q, k, v, g = attn_project(ysb, j, cos_s, sin_s, False)
            q_t = q.reshape(bs, ts, kheads, dh).transpose(0, 2, 1, 3).reshape(bs, kheads * ts, dh)

            def tiled(a, n, d):
                a = a.reshape(bs, ts, n // MAP_TILE, MAP_TILE, d).transpose(0, 2, 1, 3, 4)
                a = a.reshape(bs, n // MAP_TILE, ts * MAP_TILE, d)
                return jnp.pad(a, ((0, 0), (0, 0), (0, -(ts * MAP_TILE) % LANES), (0, 0)))

            og = _paged(q_t, cache_k, cache_v, tiled(k, kheads, dh), tiled(v, heads, dv),
                        g.reshape(bs, ts, width), page_table, lam4, subw3, j, lam_init=lam_init)
            ms = _matmul(og.reshape(bs * ts, width), attn_w_out, j, name="attn_out_proj")
            k_s.append(k.reshape(bs, ts, kheads, dh)); v_s.append(v.reshape(bs, ts, heads, dv))
        yp, ypb = _add_ln(yp, mp, ln_g3, ln_b3, i, alpha)
        ys, ysb = _add_ln(ys, ms, ln_g3, ln_b3, i, alpha)

    return (yp.reshape(bp, tp, d_model), ys.reshape(bs, ts, d_model),
            jnp.stack(conv_p), jnp.stack(ssm_p), jnp.stack(k_p), jnp.stack(v_p),
            jnp.stack(conv_s), jnp.stack(ssm_s), jnp.stack(k_s), jnp.stack(v_s))
```

```python
import functools
import math

import jax
import jax.numpy as jnp
from jax.experimental import pallas as pl
from jax.experimental.pallas import tpu as pltpu

F32 = jnp.float32
BF16 = jnp.bfloat16

NORM_EPS = 1e-5
ROPE_THETA = 10000.0
CONV_WIDTH = 4
LOG2E = 1.4426950408889634
SSM_CHUNK = 128
TAIL_ROWS = 8
LANES = 128
MAP_TILE = 8
VMEM_LIMIT = 56 * 1024 * 1024


def _cparams(sem):
    return pltpu.CompilerParams(dimension_semantics=sem, vmem_limit_bytes=VMEM_LIMIT)


def _sigmoid(x):
    return 1.0 / (1.0 + jnp.exp(-x))


def _pick(n, pref):
    if n <= pref:
        return n
    t = pref
    while n % t:
        t //= 2
    return t


def _mm_kernel(x_ref, w_ref, o_ref):
    o_ref[...] = jnp.dot(x_ref[...].astype(BF16), w_ref[...].astype(BF16),
                         preferred_element_type=F32).astype(o_ref.dtype)


def _mm_t_kernel(x_ref, w_ref, o_ref, ot_ref):
    y = jnp.dot(x_ref[...].astype(BF16), w_ref[...].astype(BF16), preferred_element_type=F32)
    o_ref[...] = y.astype(o_ref.dtype)
    ot_ref[...] = y.T.astype(ot_ref.dtype)


def _mm_rope_kernel(x_ref, w_ref, cos_ref, sin_ref, o_ref, *, scale):
    y = jnp.dot(x_ref[...].astype(BF16), w_ref[...].astype(BF16), preferred_element_type=F32)
    o_ref[...] = _rope_heads(y, cos_ref[...], sin_ref[...], scale).astype(o_ref.dtype)


RESIDENT_PANEL_BYTES = 16 * 1024 * 1024


def _mm_tiles(m, kdim, n, x_bytes, rope):
    if m <= 256:
        return m, _pick(n, 2 * RESIDENT_PANEL_BYTES // (kdim * 4 * 2)), 2
    tm = _pick(m, 2048)
    if tm * kdim * x_bytes > RESIDENT_PANEL_BYTES:
        return tm // 2, _pick(n, 256), 1
    return (tm // 2 if rope else tm), _pick(n, 256), 2


def _rope_heads(y, cos, sin, scale):
    outs = []
    for h in range(y.shape[1] // LANES):
        yh = y[:, h * LANES:(h + 1) * LANES]
        r = yh * cos + pltpu.roll(yh, LANES // 2, 1) * sin
        outs.append(r * scale if scale != 1.0 else r)
    return jnp.concatenate(outs, axis=1)


def _mm_rows_kernel(x_ref, w_ref, *refs, rope_scale):
    if rope_scale is None:
        o_ref, acc_ref = refs
    else:
        cos_ref, sin_ref, o_ref, acc_ref = refs
    k = pl.program_id(0)
    part = jnp.dot(x_ref[...].astype(BF16), w_ref[...].astype(BF16), preferred_element_type=F32)

    @pl.when(k == 0)
    def _():
        acc_ref[...] = part

    @pl.when(k > 0)
    def _():
        acc_ref[...] += part

    @pl.when(k == pl.num_programs(0) - 1)
    def _():
        y = acc_ref[...]
        if rope_scale is not None:
            y = _rope_heads(y, cos_ref[...], sin_ref[...], rope_scale)
        o_ref[...] = y.astype(o_ref.dtype)


ROW_SLAB_BYTES = 10 * 1024 * 1024


def _matmul_rows(x, w3, layer, *, n_off, n, out_dtype, rope, name):
    m, kdim = x.shape
    assert n_off % n == 0
    tk = kdim
    while tk * n * 4 > ROW_SLAB_BYTES and tk % 2 == 0 and tk > 8:
        tk //= 2
    off = n_off // n
    in_specs = [pl.BlockSpec((m, tk), lambda k: (0, k)),
                pl.BlockSpec((None, tk, n), lambda k: (layer, k, off))]
    args = [x, w3]
    if rope is not None:
        tab = pl.BlockSpec((m, LANES), lambda k: (0, 0))
        in_specs += [tab, tab]
        args += [rope[0], rope[1]]
    return pl.pallas_call(
        functools.partial(_mm_rows_kernel, rope_scale=None if rope is None else rope[2]),
        grid=(kdim // tk,),
        in_specs=in_specs,
        out_specs=pl.BlockSpec((m, n), lambda k: (0, 0)),
        out_shape=jax.ShapeDtypeStruct((m, n), out_dtype),
        scratch_shapes=[pltpu.VMEM((m, n), F32)],
        compiler_params=_cparams(("arbitrary",)),
        name=name,
    )(*args)


def _matmul(x, w3, layer, *, n_off=0, n=None, out_dtype=F32, rope=None, transposed_copy=False, name="matmul"):
    m, kdim = x.shape
    n = w3.shape[2] - n_off if n is None else n
    if m <= 256 and n_off % n == 0 and not transposed_copy:
        return _matmul_rows(x, w3, layer, n_off=n_off, n=n, out_dtype=out_dtype, rope=rope, name=name)
    tm, tn, x_buffers = _mm_tiles(m, kdim, n, x.dtype.itemsize, rope is not None or transposed_copy)
    assert n_off % tn == 0
    off = n_off // tn
    x_mode = {} if x_buffers == 2 else {"pipeline_mode": pl.Buffered(x_buffers)}
    in_specs = [pl.BlockSpec((tm, kdim), lambda i, j: (i, 0), **x_mode),
                pl.BlockSpec((None, kdim, tn), lambda i, j: (layer, 0, j + off))]
    args = [x, w3]
    body = _mm_kernel
    if rope is not None:
        cos, sin, scale = rope
        tab = pl.BlockSpec((tm, LANES), lambda i, j: (i, 0))
        in_specs += [tab, tab]
        args += [cos, sin]
        body = functools.partial(_mm_rope_kernel, scale=scale)
    out_specs = pl.BlockSpec((tm, tn), lambda i, j: (i, j))
    out_shape = jax.ShapeDtypeStruct((m, n), out_dtype)
    if transposed_copy:
        body = _mm_t_kernel
        out_specs = [out_specs, pl.BlockSpec((tn, tm), lambda i, j: (j, i))]
        out_shape = [out_shape, jax.ShapeDtypeStruct((n, m), BF16)]
    return pl.pallas_call(
        body,
        grid=(m // tm, n // tn),
        in_specs=in_specs,
        out_specs=out_specs,
        out_shape=out_shape,
        compiler_params=_cparams(("parallel", "parallel")),
        name=name,
    )(*args)


def _add_ln_kernel(x_ref, m_ref, g_ref, b_ref, o_ref, ob_ref, *, alpha):
    v = alpha * x_ref[...] + m_ref[...]
    mu = jnp.mean(v, axis=-1, keepdims=True)
    d = v - mu
    var = jnp.mean(d * d, axis=-1, keepdims=True)
    y = d * jax.lax.rsqrt(var + NORM_EPS) * g_ref[...] + b_ref[...]
    o_ref[...] = y
    ob_ref[...] = y.astype(BF16)


def _add_ln(x, mix, g3, b3, layer, alpha, *, tm=256):
    m, d = x.shape
    tm = _pick(m, tm)
    row = pl.BlockSpec((tm, d), lambda i: (i, 0))
    par = pl.BlockSpec((None, 1, d), lambda i: (layer, 0, 0))
    return pl.pallas_call(
        functools.partial(_add_ln_kernel, alpha=alpha),
        grid=(m // tm,),
        in_specs=[row, row, par, par],
        out_specs=[row, row],
        out_shape=[jax.ShapeDtypeStruct((m, d), F32), jax.ShapeDtypeStruct((m, d), BF16)],
        compiler_params=_cparams(("parallel",)),
        name="add_ln",
    )(x, mix, g3, b3)


def _ssd_kernel(*refs, valid, hg, zero_init):
    if zero_init:
        (z_ref, x_ref, b_ref, c_ref, dt_ref, cwx_ref, cwb_ref, cwc_ref, cbx_ref, cbb_ref, cbc_ref,
         dtb_ref, alog_ref, dsk_ref, nw_ref, y_ref, h_ref, ext_x, ext_b, ext_c, h_t) = refs
    else:
        (z_ref, x_ref, b_ref, c_ref, dt_ref, cwx_ref, cwb_ref, cwc_ref, cbx_ref, cbb_ref, cbc_ref,
         dtb_ref, alog_ref, dsk_ref, nw_ref, bufx_ref, bufb_ref, bufc_ref, h0_ref,
         y_ref, h_ref, ext_x, ext_b, ext_c, h_t) = refs
    L = SSM_CHUNK
    g = pl.program_id(1)
    c = pl.program_id(2)
    nc = pl.num_programs(2)
    exts = (ext_x, ext_b, ext_c)

    @pl.when(c == 0)
    def _init():
        if zero_init:
            for e in exts:
                e[0:TAIL_ROWS, :] = jnp.zeros((TAIL_ROWS, e.shape[1]), F32)
            h_t[...] = jnp.zeros(h_t.shape, F32)
        else:
            for e, buf in zip(exts, (bufx_ref, bufb_ref, bufc_ref)):
                e[...] = jnp.zeros(e.shape, F32)
                e[TAIL_ROWS - (CONV_WIDTH - 1):TAIL_ROWS, :] = buf[...]
            h_t[...] = h0_ref[...].T

    for e, raw in zip(exts, (x_ref, b_ref, c_ref)):
        e[TAIL_ROWS:TAIL_ROWS + valid, :] = raw[...]

    def conv_silu(e, cw_ref, cb_ref):
        acc = cb_ref[...] + cw_ref[CONV_WIDTH - 1:CONV_WIDTH, :] * e[TAIL_ROWS:TAIL_ROWS + L, :]
        for s in range(1, CONV_WIDTH):
            acc = acc + cw_ref[CONV_WIDTH - 1 - s:CONV_WIDTH - s, :] * e[TAIL_ROWS - s:TAIL_ROWS - s + L, :]
        return acc * _sigmoid(acc)

    xs = conv_silu(ext_x, cwx_ref, cbx_ref)
    bm = conv_silu(ext_b, cwb_ref, cbb_ref)
    cm = conv_silu(ext_c, cwc_ref, cbc_ref)

    if valid == L:
        for e in exts:
            e[0:TAIL_ROWS, :] = e[L:L + TAIL_ROWS, :]

    dtv = dt_ref[...] + dtb_ref[...]
    dt = jnp.maximum(dtv, 0.0) + jnp.log1p(jnp.exp(-jnp.abs(dtv)))
    if valid < L:
        dt = jnp.concatenate([dt, jnp.zeros((L - valid, dt.shape[1]), F32)], axis=0)
    da = dt * (-jnp.exp(alog_ref[...]) * LOG2E)
    rows = jax.lax.broadcasted_iota(jnp.int32, (L, L), 0)
    cols = jax.lax.broadcasted_iota(jnp.int32, (L, L), 1)
    causal = rows >= cols
    tril = jnp.where(causal, 1.0, 0.0).astype(BF16)
    p0 = da.astype(BF16)
    r1 = da - p0.astype(F32)
    p1 = r1.astype(BF16)
    p2 = (r1 - p1.astype(F32)).astype(BF16)
    a_cs = (jnp.dot(tril, p0, preferred_element_type=F32) + jnp.dot(tril, p1, preferred_element_type=F32)
            + jnp.dot(tril, p2, preferred_element_type=F32))

    nh = dt.shape[1]
    shift = jax.lax.rem(nh - g * hg, nh)
    a_g = pltpu.roll(a_cs, shift, 1)
    dt_g = pltpu.roll(dt, shift, 1)
    dsk_g = pltpu.roll(dsk_ref[...], shift, 1)
    a_gt = a_g.T

    cmb = cm.astype(BF16)
    cb = jax.lax.dot_general(cmb, bm.astype(BF16), (((1,), (1,)), ((), ())), preferred_element_type=F32)
    y_off = jnp.dot(cmb, h_t[...].astype(BF16), preferred_element_type=F32)

    lane = jax.lax.broadcasted_iota(jnp.int32, (L, LANES), 1)
    lo_half = lane < (LANES // 2)
    lo_half1 = lo_half[0:1, :]
    ys, xdds, cds = [], [], []
    for q in range(hg // 2):
        j0, j1 = 2 * q, 2 * q + 1
        col0 = jnp.broadcast_to(a_g[:, j0:j0 + 1], (L, LANES))
        col1 = jnp.broadcast_to(a_g[:, j1:j1 + 1], (L, LANES))
        m0 = (cb * jnp.exp2(jnp.where(causal, col0 - a_gt[j0:j0 + 1, :], -jnp.inf))).astype(BF16)
        m1 = (cb * jnp.exp2(jnp.where(causal, col1 - a_gt[j1:j1 + 1, :], -jnp.inf))).astype(BF16)
        dtp = jnp.where(lo_half, jnp.broadcast_to(dt_g[:, j0:j0 + 1], (L, LANES)),
                        jnp.broadcast_to(dt_g[:, j1:j1 + 1], (L, LANES)))
        colp = jnp.where(lo_half, col0, col1)
        dskp = jnp.where(lo_half1, jnp.broadcast_to(dsk_g[:, j0:j0 + 1], (1, LANES)),
                         jnp.broadcast_to(dsk_g[:, j1:j1 + 1], (1, LANES)))
        xs_p = xs[:, q * LANES:(q + 1) * LANES]
        xd_p = xs_p * dtp
        xd_b = xd_p.astype(BF16)
        y_diag = jnp.where(lo_half, jnp.dot(m0, xd_b, preferred_element_type=F32),
                           jnp.dot(m1, xd_b, preferred_element_type=F32))
        e_p = jnp.exp2(colp)
        ys.append(y_diag + y_off[:, q * LANES:(q + 1) * LANES] * e_p + dskp * xs_p)
        xdds.append((xd_p * jnp.exp2(colp[L - 1:L, :] - colp)).astype(BF16))
        cds.append(e_p[L - 1:L, :])
    y = jnp.concatenate(ys, axis=1)[0:valid, :]
    xdd = jnp.concatenate(xdds, axis=1)
    chunk_decay = jnp.concatenate(cds, axis=1)

    s_t = jnp.dot(bm.T.astype(BF16), xdd, preferred_element_type=F32)
    h_t[...] = h_t[...] * chunk_decay + s_t

    zg = z_ref[...]
    y = y * (zg * _sigmoid(zg))
    y = y * jax.lax.rsqrt(jnp.mean(y * y, axis=-1, keepdims=True) + NORM_EPS) * nw_ref[...]
    y_ref[...] = y.astype(y_ref.dtype)

    @pl.when(c == nc - 1)
    def _fin():
        h_ref[...] = h_t[...].T


def _ssd(zx, dt_raw, prm, layer, *, out_dtype, conv_buf=None, h0=None):
    conv_w, conv_b, dt_bias, a_log, d_skip, norm_w = prm
    nb, t, nh = dt_raw.shape
    d_inner = norm_w.shape[2]
    conv_dim = conv_w.shape[2]
    gn = (conv_dim - d_inner) // 2
    n = LANES
    groups = gn // n
    hg = nh // groups
    gw = d_inner // groups
    zero_init = conv_buf is None
    L = SSM_CHUNK
    if t >= L:
        assert t % L == 0
        valid, nc = L, t // L
    else:
        assert t % TAIL_ROWS == 0
        valid, nc = t, 1
    xo, bo, co = d_inner // gw, 2 * d_inner // n, (2 * d_inner + gn) // n

    def rowblk(width, off):
        return pl.BlockSpec((None, valid, width), lambda b, g, c: (b, c, off + g))

    def par3(rows, width, off):
        return pl.BlockSpec((None, rows, width), lambda b, g, c: (layer, 0, off + g))

    full = pl.BlockSpec((None, 1, nh), lambda b, g, c: (layer, 0, 0))
    in_specs = [rowblk(gw, 0), rowblk(gw, xo), rowblk(n, bo), rowblk(n, co),
                pl.BlockSpec((None, valid, nh), lambda b, g, c: (b, c, 0)),
                par3(CONV_WIDTH, gw, 0), par3(CONV_WIDTH, n, d_inner // n), par3(CONV_WIDTH, n, (d_inner + gn) // n),
                par3(1, gw, 0), par3(1, n, d_inner // n), par3(1, n, (d_inner + gn) // n),
                full, full, full, par3(1, gw, 0)]
    args = [zx, zx, zx, zx, dt_raw, conv_w, conv_w, conv_w, conv_b, conv_b, conv_b, dt_bias, a_log, d_skip, norm_w]
    if not zero_init:
        def bufblk(width, off):
            return pl.BlockSpec((None, CONV_WIDTH - 1, width), lambda b, g, c: (b, 0, off + g))
        in_specs += [bufblk(gw, 0), bufblk(n, d_inner // n), bufblk(n, (d_inner + gn) // n),
                     pl.BlockSpec((None, gw, n), lambda b, g, c: (b, g, 0))]
        args += [conv_buf, conv_buf, conv_buf, h0]
    y, h = pl.pallas_call(
        functools.partial(_ssd_kernel, valid=valid, hg=hg, zero_init=zero_init),
        grid=(nb, groups, nc),
        in_specs=in_specs,
        out_specs=[rowblk(gw, 0),
                   pl.BlockSpec((None, gw, n), lambda b, g, c: (b, g, 0))],
        out_shape=[jax.ShapeDtypeStruct((nb, t, d_inner), out_dtype),
                   jax.ShapeDtypeStruct((nb, d_inner, n), F32)],
        scratch_shapes=[pltpu.VMEM((L + TAIL_ROWS, gw), F32), pltpu.VMEM((L + TAIL_ROWS, n), F32),
                        pltpu.VMEM((L + TAIL_ROWS, n), F32), pltpu.VMEM((n, gw), F32)],
        compiler_params=_cparams(("parallel", "parallel", "arbitrary")),
        name="ssd",
    )(*args)
    return y, h


def _lambda_value(lam_ref, lam_init):
    s1 = jnp.sum(lam_ref[0:1, :] * lam_ref[1:2, :], axis=-1, keepdims=True)
    s2 = jnp.sum(lam_ref[2:3, :] * lam_ref[3:4, :], axis=-1, keepdims=True)
    return jnp.exp(s1) - jnp.exp(s2) + lam_init


def _norm_gate(o, g, subw, lam_init):
    o = o * jax.lax.rsqrt(jnp.mean(o * o, axis=-1, keepdims=True) + NORM_EPS) * subw * (1.0 - lam_init)
    return o * (g * _sigmoid(g))


def _diff_finish(acc1, l1, acc2, l2, lam, g, subw, lam_init):
    return _norm_gate(acc1 / l1 - lam * (acc2 / l2), g, subw, lam_init)


def _flash_kernel(q_ref, k_ref, vt_ref, g_ref, lam_ref, subw_ref, o_ref, m_ref, l_ref, acc_ref, *, dh, lam_init):
    qi = pl.program_id(2)
    ki = pl.program_id(3)
    tq, tk = q_ref.shape[0], k_ref.shape[0]

    @pl.when(ki == 0)
    def _init():
        m_ref[...] = jnp.full(m_ref.shape, -jnp.inf, F32)
        l_ref[...] = jnp.zeros(l_ref.shape, F32)
        acc_ref[...] = jnp.zeros(acc_ref.shape, F32)

    def step(masked):
        vt = vt_ref[...]
        if masked:
            keep = (jax.lax.broadcasted_iota(jnp.int32, (tk, tq), 0)
                    <= jax.lax.broadcasted_iota(jnp.int32, (tk, tq), 1))
        for c in range(2):
            st = jax.lax.dot_general(k_ref[:, c * dh:(c + 1) * dh].astype(BF16), q_ref[:, c * dh:(c + 1) * dh],
                                     (((1,), (1,)), ((), ())), preferred_element_type=F32)
            if masked:
                st = jnp.where(keep, st, -jnp.inf)
            m_old = m_ref[c]
            m_new = jnp.maximum(m_old, jnp.max(st, axis=0, keepdims=True))
            pt = jnp.exp2(st - m_new)
            corr = jnp.exp2(m_old - m_new)
            l_ref[c] = l_ref[c] * corr + jnp.sum(pt, axis=0, keepdims=True)
            acc_ref[c] = acc_ref[c] * corr + jnp.dot(vt, pt.astype(BF16), preferred_element_type=F32)
            m_ref[c] = m_new

    @pl.when(ki < qi)
    def _full():
        step(False)

    @pl.when(ki == qi)
    def _diag():
        step(True)
        lam = _lambda_value(lam_ref, lam_init)
        dt = acc_ref[0] / l_ref[0] - lam * (acc_ref[1] / l_ref[1])
        o = _norm_gate(dt.T, g_ref[...], subw_ref[...], lam_init)
        o_ref[...] = o.astype(o_ref.dtype)


def _flash(q, k, vt, g, lam4, subw3, layer, *, nb, t, heads, dh, lam_init, tq=1024):
    dv = 2 * dh
    tq = _pick(t, tq)
    nq = t // tq
    qspec = pl.BlockSpec((tq, dv), lambda b, h, i, j: (b * nq + i, h))
    return pl.pallas_call(
        functools.partial(_flash_kernel, dh=dh, lam_init=lam_init),
        grid=(nb, heads, nq, nq),
        in_specs=[qspec,
                  pl.BlockSpec((tq, dv), lambda b, h, i, j: (b * nq + jnp.minimum(i, j), h)),
                  pl.BlockSpec((dv, tq), lambda b, h, i, j: (h, b * nq + jnp.minimum(i, j))),
                  qspec,
                  pl.BlockSpec((None, 4, dh), lambda b, h, i, j: (layer, 0, 0)),
                  pl.BlockSpec((None, 1, dv), lambda b, h, i, j: (layer, 0, 0))],
        out_specs=qspec,
        out_shape=jax.ShapeDtypeStruct((nb * t, heads * dv), BF16),
        scratch_shapes=[pltpu.VMEM((2, 1, tq), F32), pltpu.VMEM((2, 1, tq), F32), pltpu.VMEM((2, dv, tq), F32)],
        compiler_params=_cparams(("parallel", "parallel", "parallel", "arbitrary")),
        name="flash_diff_attn",
    )(q, k, vt, g, lam4, subw3)


def _paged_kernel(pt_ref, q_ref, *refs, n_kt, n_vt, ts, lam_init):
    kc_ref, vc_ref, kn_ref, vn_ref, g_ref, lam_ref, subw_ref, o_ref, m_ref, l_ref, acc_ref = refs
    p = pl.program_id(1)
    n_pages = pl.num_programs(1) - 1
    g_maps = MAP_TILE
    rows_t = g_maps * ts
    kt_per_vt = n_kt // n_vt
    dv = acc_ref.shape[1]

    @pl.when(p == 0)
    def _init():
        m_ref[...] = jnp.full(m_ref.shape, -jnp.inf, F32)
        l_ref[...] = jnp.zeros(l_ref.shape, F32)
        acc_ref[...] = jnp.zeros(acc_ref.shape, F32)

    def block(k_tiles, v_tiles, causal):
        r = k_tiles[0].shape[0]
        row = jax.lax.broadcasted_iota(jnp.int32, (rows_t, r), 0)
        col = jax.lax.broadcasted_iota(jnp.int32, (rows_t, r), 1)
        q_map = row >> (ts.bit_length() - 1)
        if causal:
            q_map = jnp.where((col >> (g_maps.bit_length() - 1)) <= (row & (ts - 1)), q_map, -1)
        same_map = (col & (g_maps - 1)) == q_map
        ss = [jax.lax.dot_general(q_ref[mt * rows_t:(mt + 1) * rows_t, :], k_tiles[mt].astype(BF16),
                                  (((1,), (1,)), ((), ())), preferred_element_type=F32) for mt in range(n_kt)]
        s = jnp.where(jnp.concatenate([same_map] * n_kt, axis=0), jnp.concatenate(ss, axis=0), -jnp.inf)
        m_old = m_ref[...]
        m_new = jnp.maximum(m_old, jnp.max(s, axis=-1, keepdims=True))
        pr = jnp.exp2(s - m_new)
        corr = jnp.exp2(m_old - m_new)
        l_ref[...] = l_ref[...] * corr + jnp.sum(pr, axis=-1, keepdims=True)
        m_ref[...] = m_new
        pieces = []
        for m in range(n_kt * g_maps):
            h8 = (m % (kt_per_vt * g_maps)) // 2
            blk = pr[m * ts:(m + 1) * ts, :]
            sh = (h8 - m % g_maps) % r
            pieces.append(pltpu.roll(blk, sh, 1) if sh else blk)
        per_vt = kt_per_vt * g_maps
        pv = [jnp.dot(jnp.concatenate(pieces[ht * per_vt:(ht + 1) * per_vt], axis=0).astype(BF16),
                      v_tiles[ht].astype(BF16), preferred_element_type=F32) for ht in range(n_vt)]
        acc_ref[...] = acc_ref[...] * corr + jnp.concatenate(pv, axis=0)

    @pl.when(p < n_pages)
    def _cache():
        def tiles(ref, n):
            return [ref[:, i * g_maps:(i + 1) * g_maps, :].reshape(-1, ref.shape[-1]) for i in range(n)]

        block(tiles(kc_ref, n_kt), tiles(vc_ref, n_vt), False)

    @pl.when(p == n_pages)
    def _new():
        block([kn_ref[i] for i in range(n_kt)], [vn_ref[i] for i in range(n_vt)], True)
        lam = _lambda_value(lam_ref, lam_init)
        for h in range(n_vt * g_maps):
            r1, r2 = 2 * h * ts, (2 * h + 1) * ts
            o = _diff_finish(acc_ref[r1:r1 + ts, :], l_ref[r1:r1 + ts, :], acc_ref[r2:r2 + ts, :],
                             l_ref[r2:r2 + ts, :], lam, g_ref[:, h * dv:(h + 1) * dv], subw_ref[...], lam_init)
            o_ref[:, h * dv:(h + 1) * dv] = o


def _paged(q_t, cache_k, cache_v, k_new, v_new, g, page_table, lam4, subw3, layer, *, lam_init):
    nb, n_pages = page_table.shape
    _, _, page, maps, dh = cache_k.shape
    heads, dv = cache_v.shape[3:]
    ts, width = g.shape[1:]
    n_kt, n_vt = maps // MAP_TILE, heads // MAP_TILE
    assert ts & (ts - 1) == 0 and maps == 2 * heads and n_kt == 2 * n_vt and width == heads * dv
    last = n_pages - 1

    def cache_spec(n, d):
        return pl.BlockSpec((None, None, page, n, d),
                            lambda b, p, pt: (layer, pt[b * n_pages + jnp.minimum(p, last)], 0, 0, 0))

    def whole(a):
        return pl.BlockSpec((None,) + a.shape[1:], lambda b, p, pt: (b,) + (0,) * (a.ndim - 1))

    in_specs = ([whole(q_t), cache_spec(maps, dh), cache_spec(heads, dv)]
                + [whole(k_new), whole(v_new), whole(g),
                   pl.BlockSpec((None, 4, dh), lambda b, p, pt: (layer, 0, 0)),
                   pl.BlockSpec((None, 1, dv), lambda b, p, pt: (layer, 0, 0))])
    return pl.pallas_call(
        functools.partial(_paged_kernel, n_kt=n_kt, n_vt=n_vt, ts=ts, lam_init=lam_init),
        grid_spec=pltpu.PrefetchScalarGridSpec(
            num_scalar_prefetch=1,
            grid=(nb, n_pages + 1),
            in_specs=in_specs,
            out_specs=pl.BlockSpec((None, ts, width), lambda b, p, pt: (b, 0, 0)),
            scratch_shapes=[pltpu.VMEM((maps * ts, 1), F32), pltpu.VMEM((maps * ts, 1), F32),
                            pltpu.VMEM((maps * ts, dv), F32)]),
        out_shape=jax.ShapeDtypeStruct((nb, ts, width), F32),
        compiler_params=_cparams(("parallel", "arbitrary")),
        name="paged_diff_attn",
    )(page_table.reshape(-1), q_t, cache_k, cache_v, k_new, v_new, g, lam4, subw3)


def _rope_tables(pos, dh):
    half = dh // 2
    inv = ROPE_THETA ** (-jnp.arange(half, dtype=F32) / half)
    ang = pos.astype(F32)[:, None] * inv[None]
    cos, sin = jnp.cos(ang), jnp.sin(ang)
    return jnp.concatenate([cos, cos], -1), jnp.concatenate([-sin, sin], -1)


def kernel(x_prompt, x_sample, state_conv, state_ssm, cache_k, cache_v, page_table, ssm_w_in, ssm_conv_w, ssm_conv_b, ssm_dt_bias, ssm_a_log, ssm_d, ssm_norm_w, ssm_w_out, attn_w_in, attn_lambda_q1, attn_lambda_k1, attn_lambda_q2, attn_lambda_k2, attn_subln_w, attn_w_out, ln_g, ln_b):
    bp, tp, d_model = x_prompt.shape
    bs, ts, _ = x_sample.shape
    depth = ln_g.shape[0]
    n_mixers = 2
    alpha = (2 * depth) ** 0.25
    past_len = page_table.shape[1] * cache_k.shape[2]

    _, _, nh, hp, n_state = state_ssm.shape
    d_inner = nh * hp
    conv_dim = state_conv.shape[-1]
    zx_w = d_inner + conv_dim
    n_pool, page, kheads, dh = cache_k.shape[1:]
    heads = kheads // 2
    dv = 2 * dh
    qk_dim = kheads * dh
    width = heads * dv
    assert n_state == LANES and dh == LANES and 2 * hp == LANES and qk_dim == width

    ln_g3, ln_b3 = ln_g.reshape(depth, 1, d_model), ln_b.reshape(depth, 1, d_model)
    ssm_prm = (ssm_conv_w, ssm_conv_b.reshape(-1, 1, conv_dim), ssm_dt_bias.reshape(-1, 1, nh),
               ssm_a_log.reshape(-1, 1, nh), ssm_d.reshape(-1, 1, nh), ssm_norm_w.reshape(-1, 1, d_inner))
    lam4 = jnp.stack([attn_lambda_q1, attn_lambda_k1, attn_lambda_q2, attn_lambda_k2], axis=1)
    subw3 = attn_subln_w.reshape(-1, 1, dv)
    state_ssm4 = state_ssm.reshape(state_ssm.shape[0], bs, d_inner, n_state)

    cos_p, sin_p = _rope_tables(jnp.tile(jnp.arange(tp), bp), dh)
    cos_s, sin_s = _rope_tables(jnp.tile(past_len + jnp.arange(ts), bs), dh)

    yp = x_prompt.reshape(bp * tp, d_model)
    ys = x_sample.reshape(bs * ts, d_model)
    ypb, ysb = yp.astype(BF16), ys.astype(BF16)
    conv_p, ssm_p, k_p, v_p, conv_s, ssm_s, k_s, v_s = ([] for _ in range(8))

    def ssm_layer(xb, j, nb, t, conv_buf, h0):
        zx = _matmul(xb, ssm_w_in, j, n=zx_w, name="ssm_in_proj").reshape(nb, t, zx_w)
        dt_raw = _matmul(xb, ssm_w_in, j, n_off=zx_w, n=nh, name="ssm_dt_proj").reshape(nb, t, nh)
        y, h = _ssd(zx, dt_raw, ssm_prm, j, out_dtype=BF16 if t >= SSM_CHUNK else F32, conv_buf=conv_buf, h0=h0)
        mix = _matmul(y.reshape(nb * t, d_inner), ssm_w_out, j, name="ssm_out_proj")
        assert t >= CONV_WIDTH - 1
        return mix, zx[:, t - (CONV_WIDTH - 1):, d_inner:], h.reshape(nb, nh, hp, n_state)

    def attn_project(xb, j, cos, sin, v_transposed):
        q = _matmul(xb, attn_w_in, j, n=qk_dim, out_dtype=BF16, rope=(cos, sin, dh ** -0.5 * LOG2E),
                    name="attn_q_proj")
        k = _matmul(xb, attn_w_in, j, n_off=qk_dim, n=qk_dim, rope=(cos, sin, 1.0), name="attn_k_proj")
        v = _matmul(xb, attn_w_in, j, n_off=2 * qk_dim, n=width, transposed_copy=v_transposed, name="attn_v_proj")
        g = _matmul(xb, attn_w_in, j, n_off=2 * qk_dim + width, n=width, name="attn_g_proj")
        return q, k, v, g

    for i in range(depth):
        j = i // n_mixers
        if i % n_mixers == 0:
            mp, cb, h = ssm_layer(ypb, j, bp, tp, None, None)
            conv_p.append(cb); ssm_p.append(h)
            ms, cb, h = ssm_layer(ysb, j, bs, ts, state_conv[j], state_ssm4[j])
            conv_s.append(cb); ssm_s.append(h)
        else:
            lam_init = 0.8 - 0.6 * math.exp(-0.3 * i)
            q, k, (v, vt), g = attn_project(ypb, j, cos_p, sin_p, True)
            og = _flash(q, k, vt, g, lam4, subw3, j, nb=bp, t=tp, heads=heads, dh=dh, lam_init=lam_init)
            mp = _matmul(og, attn_w_out, j, name="attn_out_proj")
            k_p.append(k.reshape(bp, tp, kheads, dh)); v_p.append(v.reshape(bp, tp, heads, dv))

            q, k, v, g = attn_project(ysb, j, cos_s, sin_s, False)
            q_t = q.reshape(bs, ts, kheads, dh).transpose(0, 2, 1, 3).reshape(bs, kheads * ts, dh)

            def tiled(a, n, d):
                a = a.reshape(bs, ts, n // MAP_TILE, MAP_TILE, d).transpose(0, 2, 1, 3, 4)
                a = a.reshape(bs, n // MAP_TILE, ts * MAP_TILE, d)
                return jnp.pad(a, ((0, 0), (0, 0), (0, -(ts * MAP_TILE) % LANES), (0, 0)))

            og = _paged(q_t, cache_k, cache_v, tiled(k, kheads, dh), tiled(v, heads, dv),
                        g.reshape(bs, ts, width), page_table, lam4, subw3, j, lam_init=lam_init)
            ms = _matmul(og.reshape(bs * ts, width), attn_w_out, j, name="attn_out_proj")
            k_s.append(k.reshape(bs, ts, kheads, dh)); v_s.append(v.reshape(bs, ts, heads, dv))
        yp, ypb = _add_ln(yp, mp, ln_g3, ln_b3, i, alpha)
        ys, ysb = _add_ln(ys, ms, ln_g3, ln_b3, i, alpha)

    return (yp.reshape(bp, tp, d_model), ys.reshape(bs, ts, d_model),
            jnp.stack(conv_p), jnp.stack(ssm_p), jnp.stack(k_p), jnp.stack(v_p),
            jnp.stack(conv_s), jnp.stack(ssm_s), jnp.stack(k_s), jnp.stack(v_s))
```

```python
import functools
import math

import jax
import jax.numpy as jnp
from jax.experimental import pallas as pl
from jax.experimental.pallas import tpu as pltpu

F32 = jnp.float32
BF16 = jnp.bfloat16

NORM_EPS = 1e-5
ROPE_THETA = 10000.0
CONV_WIDTH = 4
LOG2E = 1.4426950408889634
SSM_CHUNK = 128
TAIL_ROWS = 8
LANES = 128
MAP_TILE = 8
PAGE_SPLITS = 4
ROW_STREAMS = 4
VMEM_LIMIT = 56 * 1024 * 1024


def _cparams(sem):
    return pltpu.CompilerParams(dimension_semantics=sem, vmem_limit_bytes=VMEM_LIMIT)


def _sigmoid(x):
    return 1.0 / (1.0 + jnp.exp(-x))


def _pick(n, pref):
    if n <= pref:
        return n
    t = pref
    while n % t:
        t //= 2
    return t


def _mm_kernel(x_ref, w_ref, o_ref):
    o_ref[...] = jnp.dot(x_ref[...].astype(BF16), w_ref[...].astype(BF16),
                         preferred_element_type=F32).astype(o_ref.dtype)


def _mm_t_kernel(x_ref, w_ref, o_ref, ot_ref):
    y = jnp.dot(x_ref[...].astype(BF16), w_ref[...].astype(BF16), preferred_element_type=F32)
    o_ref[...] = y.astype(o_ref.dtype)
    ot_ref[...] = y.T.astype(ot_ref.dtype)


def _rope_heads(y, cos, sin, scale):
    outs = []
    for h in range(y.shape[1] // LANES):
        yh = y[:, h * LANES:(h + 1) * LANES]
        r = yh * cos + pltpu.roll(yh, LANES // 2, 1) * sin
        outs.append(r * scale if scale != 1.0 else r)
    return jnp.concatenate(outs, axis=1)


def _mm_rope_kernel(x_ref, w_ref, cos_ref, sin_ref, o_ref, *, scale):
    y = jnp.dot(x_ref[...].astype(BF16), w_ref[...].astype(BF16), preferred_element_type=F32)
    o_ref[...] = _rope_heads(y, cos_ref[...], sin_ref[...], scale).astype(o_ref.dtype)


RESIDENT_PANEL_BYTES = 16 * 1024 * 1024


def _mm_tiles(m, kdim, n, x_bytes, epilogue):
    tm = _pick(m, 2048)
    if tm * kdim * x_bytes > RESIDENT_PANEL_BYTES:
        return tm // 2, _pick(n, 256), 1
    return (tm // 2 if epilogue else tm), _pick(n, 256), 2


def _mm_rows_kernel(x_ref, *refs, rope_scale, streams):
    w_refs, refs = refs[:streams], refs[streams:]
    if rope_scale is None:
        o_ref, acc_ref = refs
    else:
        cos_ref, sin_ref, o_ref, acc_ref = refs
    k = pl.program_id(0)
    w = jnp.concatenate([w_refs[r][...] for r in range(streams)], axis=0) if streams > 1 else w_refs[0][...]
    part = jnp.dot(x_ref[...].astype(BF16), w.astype(BF16), preferred_element_type=F32)

    @pl.when(k == 0)
    def _():
        acc_ref[...] = part

    @pl.when(k > 0)
    def _():
        acc_ref[...] += part

    @pl.when(k == pl.num_programs(0) - 1)
    def _():
        y = acc_ref[...]
        if rope_scale is not None:
            y = _rope_heads(y, cos_ref[...], sin_ref[...], rope_scale)
        o_ref[...] = y.astype(o_ref.dtype)


ROW_SLAB_BYTES = 10 * 1024 * 1024


def _matmul_rows(x, w3, layer, *, n_off, n, out_dtype, rope, name):
    m, kdim = x.shape
    assert n_off % n == 0
    tk = kdim
    while tk * n * 4 > ROW_SLAB_BYTES and tk % 2 == 0 and tk > 8:
        tk //= 2
    off = n_off // n
    streams = ROW_STREAMS if tk % (ROW_STREAMS * TAIL_ROWS) == 0 else 1
    rk = tk // streams
    in_specs = [pl.BlockSpec((m, tk), lambda k: (0, k))]
    in_specs += [pl.BlockSpec((None, rk, n), lambda k, r=r: (layer, k * streams + r, off)) for r in range(streams)]
    args = [x] + [w3] * streams
    if rope is not None:
        tab = pl.BlockSpec((m, LANES), lambda k: (0, 0))
        in_specs += [tab, tab]
        args += [rope[0], rope[1]]
    return pl.pallas_call(
        functools.partial(_mm_rows_kernel, rope_scale=None if rope is None else rope[2], streams=streams),
        grid=(kdim // tk,),
        in_specs=in_specs,
        out_specs=pl.BlockSpec((m, n), lambda k: (0, 0)),
        out_shape=jax.ShapeDtypeStruct((m, n), out_dtype),
        scratch_shapes=[pltpu.VMEM((m, n), F32)],
        compiler_params=_cparams(("arbitrary",)),
        name=name,
    )(*args)


def _matmul(x, w3, layer, *, n_off=0, n=None, out_dtype=F32, rope=None, transposed_copy=False, name="matmul"):
    m, kdim = x.shape
    n = w3.shape[2] - n_off if n is None else n
    if m <= 256 and not transposed_copy:
        return _matmul_rows(x, w3, layer, n_off=n_off, n=n, out_dtype=out_dtype, rope=rope, name=name)
    tm, tn, x_buffers = _mm_tiles(m, kdim, n, x.dtype.itemsize, rope is not None or transposed_copy)
    assert n_off % tn == 0
    off = n_off // tn
    x_mode = {} if x_buffers == 2 else {"pipeline_mode": pl.Buffered(x_buffers)}
    in_specs = [pl.BlockSpec((tm, kdim), lambda i, j: (i, 0), **x_mode),
                pl.BlockSpec((None, kdim, tn), lambda i, j: (layer, 0, j + off))]
    args = [x, w3]
    body = _mm_kernel
    if rope is not None:
        cos, sin, scale = rope
        tab = pl.BlockSpec((tm, LANES), lambda i, j: (i, 0))
        in_specs += [tab, tab]
        args += [cos, sin]
        body = functools.partial(_mm_rope_kernel, scale=scale)
    out_specs = pl.BlockSpec((tm, tn), lambda i, j: (i, j))
    out_shape = jax.ShapeDtypeStruct((m, n), out_dtype)
    if transposed_copy:
        body = _mm_t_kernel
        out_specs = [out_specs, pl.BlockSpec((tn, tm), lambda i, j: (j, i))]
        out_shape = [out_shape, jax.ShapeDtypeStruct((n, m), BF16)]
    return pl.pallas_call(
        body,
        grid=(m // tm, n // tn),
        in_specs=in_specs,
        out_specs=out_specs,
        out_shape=out_shape,
        compiler_params=_cparams(("parallel", "parallel")),
        name=name,
    )(*args)


def _add_ln_kernel(x_ref, m_ref, g_ref, b_ref, o_ref, ob_ref, *, alpha):
    v = alpha * x_ref[...] + m_ref[...]
    mu = jnp.mean(v, axis=-1, keepdims=True)
    d = v - mu
    var = jnp.mean(d * d, axis=-1, keepdims=True)
    y = d * jax.lax.rsqrt(var + NORM_EPS) * g_ref[...] + b_ref[...]
    o_ref[...] = y
    ob_ref[...] = y.astype(BF16)


def _add_ln(x, mix, g3, b3, layer, alpha, *, tm=256):
    m, d = x.shape
    tm = _pick(m, tm)
    row = pl.BlockSpec((tm, d), lambda i: (i, 0))
    par = pl.BlockSpec((None, 1, d), lambda i: (layer, 0, 0))
    return pl.pallas_call(
        functools.partial(_add_ln_kernel, alpha=alpha),
        grid=(m // tm,),
        in_specs=[row, row, par, par],
        out_specs=[row, row],
        out_shape=[jax.ShapeDtypeStruct((m, d), F32), jax.ShapeDtypeStruct((m, d), BF16)],
        compiler_params=_cparams(("parallel",)),
        name="add_ln",
    )(x, mix, g3, b3)


def _ssd_kernel(*refs, valid, hg, zero_init):
    if zero_init:
        (z_ref, x_ref, b_ref, c_ref, dt_ref, cwx_ref, cwb_ref, cwc_ref, cbx_ref, cbb_ref, cbc_ref,
         dtb_ref, alog_ref, dsk_ref, nw_ref, y_ref, h_ref, ext_x, ext_b, ext_c, h_t) = refs
    else:
        (z_ref, x_ref, b_ref, c_ref, dt_ref, cwx_ref, cwb_ref, cwc_ref, cbx_ref, cbb_ref, cbc_ref,
         dtb_ref, alog_ref, dsk_ref, nw_ref, bufx_ref, bufb_ref, bufc_ref, h0_ref,
         y_ref, h_ref, ext_x, ext_b, ext_c, h_t) = refs
    L = SSM_CHUNK
    g = pl.program_id(1)
    c = pl.program_id(2)
    nc = pl.num_programs(2)
    exts = (ext_x, ext_b, ext_c)

    @pl.when(c == 0)
    def _init():
        if zero_init:
            for e in exts:
                e[0:TAIL_ROWS, :] = jnp.zeros((TAIL_ROWS, e.shape[1]), F32)
            h_t[...] = jnp.zeros(h_t.shape, F32)
        else:
            for e, buf in zip(exts, (bufx_ref, bufb_ref, bufc_ref)):
                e[...] = jnp.zeros(e.shape, F32)
                e[TAIL_ROWS - (CONV_WIDTH - 1):TAIL_ROWS, :] = buf[...]
            h_t[...] = h0_ref[...].T

    for e, raw in zip(exts, (x_ref, b_ref, c_ref)):
        e[TAIL_ROWS:TAIL_ROWS + valid, :] = raw[...]

    def conv_silu(e, cw_ref, cb_ref):
        acc = cb_ref[...] + cw_ref[CONV_WIDTH - 1:CONV_WIDTH, :] * e[TAIL_ROWS:TAIL_ROWS + L, :]
        for s in range(1, CONV_WIDTH):
            acc = acc + cw_ref[CONV_WIDTH - 1 - s:CONV_WIDTH - s, :] * e[TAIL_ROWS - s:TAIL_ROWS - s + L, :]
        return acc * _sigmoid(acc)

    xs = conv_silu(ext_x, cwx_ref, cbx_ref)
    bm = conv_silu(ext_b, cwb_ref, cbb_ref)
    cm = conv_silu(ext_c, cwc_ref, cbc_ref)

    if valid == L:
        for e in exts:
            e[0:TAIL_ROWS, :] = e[L:L + TAIL_ROWS, :]

    dtv = dt_ref[...] + dtb_ref[...]
    dt = jnp.maximum(dtv, 0.0) + jnp.log1p(jnp.exp(-jnp.abs(dtv)))
    if valid < L:
        dt = jnp.concatenate([dt, jnp.zeros((L - valid, dt.shape[1]), F32)], axis=0)
    da = dt * (-jnp.exp(alog_ref[...]) * LOG2E)
    rows = jax.lax.broadcasted_iota(jnp.int32, (L, L), 0)
    cols = jax.lax.broadcasted_iota(jnp.int32, (L, L), 1)
    causal = rows >= cols
    tril = jnp.where(causal, 1.0, 0.0).astype(BF16)
    p0 = da.astype(BF16)
    r1 = da - p0.astype(F32)
    p1 = r1.astype(BF16)
    p2 = (r1 - p1.astype(F32)).astype(BF16)
    a_cs = (jnp.dot(tril, p0, preferred_element_type=F32) + jnp.dot(tril, p1, preferred_element_type=F32)
            + jnp.dot(tril, p2, preferred_element_type=F32))

    nh = dt.shape[1]
    shift = jax.lax.rem(nh - g * hg, nh)
    a_g = pltpu.roll(a_cs, shift, 1)
    dt_g = pltpu.roll(dt, shift, 1)
    dsk_g = pltpu.roll(dsk_ref[...], shift, 1)
    a_gt = a_g.T

    cmb = cm.astype(BF16)
    cb = jax.lax.dot_general(cmb, bm.astype(BF16), (((1,), (1,)), ((), ())), preferred_element_type=F32)
    y_off = jnp.dot(cmb, h_t[...].astype(BF16), preferred_element_type=F32)

    lane = jax.lax.broadcasted_iota(jnp.int32, (L, LANES), 1)
    lo_half = lane < (LANES // 2)
    lo_half1 = lo_half[0:1, :]
    ys, xdds, cds = [], [], []
    for q in range(hg // 2):
        j0, j1 = 2 * q, 2 * q + 1
        col0 = jnp.broadcast_to(a_g[:, j0:j0 + 1], (L, LANES))
        col1 = jnp.broadcast_to(a_g[:, j1:j1 + 1], (L, LANES))
        m0 = (cb * jnp.exp2(jnp.where(causal, col0 - a_gt[j0:j0 + 1, :], -jnp.inf))).astype(BF16)
        m1 = (cb * jnp.exp2(jnp.where(causal, col1 - a_gt[j1:j1 + 1, :], -jnp.inf))).astype(BF16)
        dtp = jnp.where(lo_half, jnp.broadcast_to(dt_g[:, j0:j0 + 1], (L, LANES)),
                        jnp.broadcast_to(dt_g[:, j1:j1 + 1], (L, LANES)))
        colp = jnp.where(lo_half, col0, col1)
        dskp = jnp.where(lo_half1, jnp.broadcast_to(dsk_g[:, j0:j0 + 1], (1, LANES)),
                         jnp.broadcast_to(dsk_g[:, j1:j1 + 1], (1, LANES)))
        xs_p = xs[:, q * LANES:(q + 1) * LANES]
        xd_p = xs_p * dtp
        xd_b = xd_p.astype(BF16)
        y_diag = jnp.where(lo_half, jnp.dot(m0, xd_b, preferred_element_type=F32),
                           jnp.dot(m1, xd_b, preferred_element_type=F32))
        e_p = jnp.exp2(colp)
        ys.append(y_diag + y_off[:, q * LANES:(q + 1) * LANES] * e_p + dskp * xs_p)
        xdds.append((xd_p * jnp.exp2(colp[L - 1:L, :] - colp)).astype(BF16))
        cds.append(e_p[L - 1:L, :])
    y = jnp.concatenate(ys, axis=1)[0:valid, :]
    xdd = jnp.concatenate(xdds, axis=1)
    chunk_decay = jnp.concatenate(cds, axis=1)

    s_t = jnp.dot(bm.T.astype(BF16), xdd, preferred_element_type=F32)
    h_t[...] = h_t[...] * chunk_decay + s_t

    zg = z_ref[...]
    y = y * (zg * _sigmoid(zg))
    y = y * jax.lax.rsqrt(jnp.mean(y * y, axis=-1, keepdims=True) + NORM_EPS) * nw_ref[...]
    y_ref[...] = y.astype(y_ref.dtype)

    @pl.when(c == nc - 1)
    def _fin():
        h_ref[...] = h_t[...].T


def _ssd(zx, dt_raw, prm, layer, *, out_dtype, conv_buf=None, h0=None):
    conv_w, conv_b, dt_bias, a_log, d_skip, norm_w = prm
    nb, t, nh = dt_raw.shape
    d_inner = norm_w.shape[2]
    conv_dim = conv_w.shape[2]
    gn = (conv_dim - d_inner) // 2
    n = LANES
    groups = gn // n
    hg = nh // groups
    gw = d_inner // groups
    zero_init = conv_buf is None
    L = SSM_CHUNK
    if t >= L:
        assert t % L == 0
        valid, nc = L, t // L
    else:
        assert t % TAIL_ROWS == 0
        valid, nc = t, 1
    xo, bo, co = d_inner // gw, 2 * d_inner // n, (2 * d_inner + gn) // n

    def rowblk(width, off):
        return pl.BlockSpec((None, valid, width), lambda b, g, c: (b, c, off + g))

    def par3(rows, width, off):
        return pl.BlockSpec((None, rows, width), lambda b, g, c: (layer, 0, off + g))

    full = pl.BlockSpec((None, 1, nh), lambda b, g, c: (layer, 0, 0))
    in_specs = [rowblk(gw, 0), rowblk(gw, xo), rowblk(n, bo), rowblk(n, co),
                pl.BlockSpec((None, valid, nh), lambda b, g, c: (b, c, 0)),
                par3(CONV_WIDTH, gw, 0), par3(CONV_WIDTH, n, d_inner // n), par3(CONV_WIDTH, n, (d_inner + gn) // n),
                par3(1, gw, 0), par3(1, n, d_inner // n), par3(1, n, (d_inner + gn) // n),
                full, full, full, par3(1, gw, 0)]
    args = [zx, zx, zx, zx, dt_raw, conv_w, conv_w, conv_w, conv_b, conv_b, conv_b, dt_bias, a_log, d_skip, norm_w]
    if not zero_init:
        def bufblk(width, off):
            return pl.BlockSpec((None, CONV_WIDTH - 1, width), lambda b, g, c: (b, 0, off + g))
        in_specs += [bufblk(gw, 0), bufblk(n, d_inner // n), bufblk(n, (d_inner + gn) // n),
                     pl.BlockSpec((None, gw, n), lambda b, g, c: (b, g, 0))]
        args += [conv_buf, conv_buf, conv_buf, h0]
    y, h = pl.pallas_call(
        functools.partial(_ssd_kernel, valid=valid, hg=hg, zero_init=zero_init),
        grid=(nb, groups, nc),
        in_specs=in_specs,
        out_specs=[rowblk(gw, 0),
                   pl.BlockSpec((None, gw, n), lambda b, g, c: (b, g, 0))],
        out_shape=[jax.ShapeDtypeStruct((nb, t, d_inner), out_dtype),
                   jax.ShapeDtypeStruct((nb, d_inner, n), F32)],
        scratch_shapes=[pltpu.VMEM((L + TAIL_ROWS, gw), F32), pltpu.VMEM((L + TAIL_ROWS, n), F32),
                        pltpu.VMEM((L + TAIL_ROWS, n), F32), pltpu.VMEM((n, gw), F32)],
        compiler_params=_cparams(("parallel", "parallel", "arbitrary")),
        name="ssd",
    )(*args)
    return y, h


def _lambda_value(lam_ref, lam_init):
    s1 = jnp.sum(lam_ref[0:1, :] * lam_ref[1:2, :], axis=-1, keepdims=True)
    s2 = jnp.sum(lam_ref[2:3, :] * lam_ref[3:4, :], axis=-1, keepdims=True)
    return jnp.exp(s1) - jnp.exp(s2) + lam_init


def _norm_gate(o, g, subw, lam_init):
    o = o * jax.lax.rsqrt(jnp.mean(o * o, axis=-1, keepdims=True) + NORM_EPS) * subw * (1.0 - lam_init)
    return o * (g * _sigmoid(g))


def _diff_finish(acc1, l1, acc2, l2, lam, g, subw, lam_init):
    return _norm_gate(acc1 / l1 - lam * (acc2 / l2), g, subw, lam_init)


def _flash_kernel(q_ref, k_ref, vt_ref, g_ref, lam_ref, subw_ref, o_ref, m_ref, l_ref, acc_ref, *, dh, lam_init):
    qi = pl.program_id(2)
    ki = pl.program_id(3)
    tq, tk = q_ref.shape[0], k_ref.shape[0]

    @pl.when(ki == 0)
    def _init():
        m_ref[...] = jnp.full(m_ref.shape, -jnp.inf, F32)
        l_ref[...] = jnp.zeros(l_ref.shape, F32)
        acc_ref[...] = jnp.zeros(acc_ref.shape, F32)

    def step(masked):
        vt = vt_ref[...]
        if masked:
            keep = (jax.lax.broadcasted_iota(jnp.int32, (tk, tq), 0)
                    <= jax.lax.broadcasted_iota(jnp.int32, (tk, tq), 1))
        for c in range(2):
            st = jax.lax.dot_general(k_ref[:, c * dh:(c + 1) * dh].astype(BF16), q_ref[:, c * dh:(c + 1) * dh],
                                     (((1,), (1,)), ((), ())), preferred_element_type=F32)
            if masked:
                st = jnp.where(keep, st, -jnp.inf)
            m_old = m_ref[c]
            m_new = jnp.maximum(m_old, jnp.max(st, axis=0, keepdims=True))
            pt = jnp.exp2(st - m_new)
            corr = jnp.exp2(m_old - m_new)
            l_ref[c] = l_ref[c] * corr + jnp.sum(pt, axis=0, keepdims=True)
            acc_ref[c] = acc_ref[c] * corr + jnp.dot(vt, pt.astype(BF16), preferred_element_type=F32)
            m_ref[c] = m_new

    @pl.when(ki < qi)
    def _full():
        step(False)

    @pl.when(ki == qi)
    def _diag():
        step(True)
        lam = _lambda_value(lam_ref, lam_init)
        dt = acc_ref[0] / l_ref[0] - lam * (acc_ref[1] / l_ref[1])
        o = _norm_gate(dt.T, g_ref[...], subw_ref[...], lam_init)
        o_ref[...] = o.astype(o_ref.dtype)


def _flash(q, k, vt, g, lam4, subw3, layer, *, nb, t, heads, dh, lam_init, tq=1024):
    dv = 2 * dh
    tq = _pick(t, tq)
    nq = t // tq
    qspec = pl.BlockSpec((tq, dv), lambda b, h, i, j: (b * nq + i, h))
    return pl.pallas_call(
        functools.partial(_flash_kernel, dh=dh, lam_init=lam_init),
        grid=(nb, heads, nq, nq),
        in_specs=[qspec,
                  pl.BlockSpec((tq, dv), lambda b, h, i, j: (b * nq + jnp.minimum(i, j), h)),
                  pl.BlockSpec((dv, tq), lambda b, h, i, j: (h, b * nq + jnp.minimum(i, j))),
                  qspec,
                  pl.BlockSpec((None, 4, dh), lambda b, h, i, j: (layer, 0, 0)),
                  pl.BlockSpec((None, 1, dv), lambda b, h, i, j: (layer, 0, 0))],
        out_specs=qspec,
        out_shape=jax.ShapeDtypeStruct((nb * t, heads * dv), BF16),
        scratch_shapes=[pltpu.VMEM((2, 1, tq), F32), pltpu.VMEM((2, 1, tq), F32), pltpu.VMEM((2, dv, tq), F32)],
        compiler_params=_cparams(("parallel", "parallel", "parallel", "arbitrary")),
        name="flash_diff_attn",
    )(q, k, vt, g, lam4, subw3)


def _paged_kernel(pt_ref, q_ref, *refs, n_kt, n_vt, ts, lam_init):
    kc_refs, vc_refs = refs[:PAGE_SPLITS], refs[PAGE_SPLITS:2 * PAGE_SPLITS]
    kn_ref, vn_ref, g_ref, lam_ref, subw_ref, o_ref, m_ref, l_ref, acc_ref = refs[2 * PAGE_SPLITS:]
    p = pl.program_id(1)
    n_pages = pl.num_programs(1) - 1
    g_maps = MAP_TILE
    rows_t = g_maps * ts
    kt_per_vt = n_kt // n_vt
    dv = acc_ref.shape[1]

    @pl.when(p == 0)
    def _init():
        m_ref[...] = jnp.full(m_ref.shape, -jnp.inf, F32)
        l_ref[...] = jnp.zeros(l_ref.shape, F32)
        acc_ref[...] = jnp.zeros(acc_ref.shape, F32)

    def block(k_tiles, v_tiles, causal):
        r = k_tiles[0].shape[0]
        row = jax.lax.broadcasted_iota(jnp.int32, (rows_t, r), 0)
        col = jax.lax.broadcasted_iota(jnp.int32, (rows_t, r), 1)
        q_map = row >> (ts.bit_length() - 1)
        if causal:
            q_map = jnp.where((col >> (g_maps.bit_length() - 1)) <= (row & (ts - 1)), q_map, -1)
        same_map = (col & (g_maps - 1)) == q_map
        ss = [jax.lax.dot_general(q_ref[mt * rows_t:(mt + 1) * rows_t, :], k_tiles[mt].astype(BF16),
                                  (((1,), (1,)), ((), ())), preferred_element_type=F32) for mt in range(n_kt)]
        s = jnp.where(jnp.concatenate([same_map] * n_kt, axis=0), jnp.concatenate(ss, axis=0), -jnp.inf)
        m_old = m_ref[...]
        m_new = jnp.maximum(m_old, jnp.max(s, axis=-1, keepdims=True))
        pr = jnp.exp2(s - m_new)
        corr = jnp.exp2(m_old - m_new)
        l_ref[...] = l_ref[...] * corr + jnp.sum(pr, axis=-1, keepdims=True)
        m_ref[...] = m_new
        pieces = []
        for m in range(n_kt * g_maps):
            h8 = (m % (kt_per_vt * g_maps)) // 2
            blk = pr[m * ts:(m + 1) * ts, :]
            sh = (h8 - m % g_maps) % r
            pieces.append(pltpu.roll(blk, sh, 1) if sh else blk)
        per_vt = kt_per_vt * g_maps
        pv = [jnp.dot(jnp.concatenate(pieces[ht * per_vt:(ht + 1) * per_vt], axis=0).astype(BF16),
                      v_tiles[ht].astype(BF16), preferred_element_type=F32) for ht in range(n_vt)]
        acc_ref[...] = acc_ref[...] * corr + jnp.concatenate(pv, axis=0)

    @pl.when(p < n_pages)
    def _cache():
        def tiles(parts, n):
            return [jnp.concatenate([r[:, i * g_maps:(i + 1) * g_maps, :].reshape(-1, r.shape[-1]) for r in parts],
                                    axis=0) for i in range(n)]

        block(tiles(kc_refs, n_kt), tiles(vc_refs, n_vt), False)

    @pl.when(p == n_pages)
    def _new():
        block([kn_ref[i] for i in range(n_kt)], [vn_ref[i] for i in range(n_vt)], True)
        lam = _lambda_value(lam_ref, lam_init)
        for h in range(n_vt * g_maps):
            r1, r2 = 2 * h * ts, (2 * h + 1) * ts
            o = _diff_finish(acc_ref[r1:r1 + ts, :], l_ref[r1:r1 + ts, :], acc_ref[r2:r2 + ts, :],
                             l_ref[r2:r2 + ts, :], lam, g_ref[:, h * dv:(h + 1) * dv], subw_ref[...], lam_init)
            o_ref[:, h * dv:(h + 1) * dv] = o


def _paged(q_t, cache_k, cache_v, k_new, v_new, g, page_table, lam4, subw3, layer, *, lam_init):
    nb, n_pages = page_table.shape
    _, _, page, maps, dh = cache_k.shape
    heads, dv = cache_v.shape[3:]
    ts, width = g.shape[1:]
    n_kt, n_vt = maps // MAP_TILE, heads // MAP_TILE
    assert ts & (ts - 1) == 0 and maps == 2 * heads and n_kt == 2 * n_vt and width == heads * dv
    assert page % (PAGE_SPLITS * TAIL_ROWS) == 0
    last = n_pages - 1

    def cache_specs(n, d):
        return [pl.BlockSpec((None, None, page // PAGE_SPLITS, n, d),
                             lambda b, p, pt, part=part: (layer, pt[b * n_pages + jnp.minimum(p, last)], part, 0, 0))
                for part in range(PAGE_SPLITS)]

    def whole(a):
        return pl.BlockSpec((None,) + a.shape[1:], lambda b, p, pt: (b,) + (0,) * (a.ndim - 1))

    in_specs = ([whole(q_t)] + cache_specs(maps, dh) + cache_specs(heads, dv)
                + [whole(k_new), whole(v_new), whole(g),
                   pl.BlockSpec((None, 4, dh), lambda b, p, pt: (layer, 0, 0)),
                   pl.BlockSpec((None, 1, dv), lambda b, p, pt: (layer, 0, 0))])
    return pl.pallas_call(
        functools.partial(_paged_kernel, n_kt=n_kt, n_vt=n_vt, ts=ts, lam_init=lam_init),
        grid_spec=pltpu.PrefetchScalarGridSpec(
            num_scalar_prefetch=1,
            grid=(nb, n_pages + 1),
            in_specs=in_specs,
            out_specs=pl.BlockSpec((None, ts, width), lambda b, p, pt: (b, 0, 0)),
            scratch_shapes=[pltpu.VMEM((maps * ts, 1), F32), pltpu.VMEM((maps * ts, 1), F32),
                            pltpu.VMEM((maps * ts, dv), F32)]),
        out_shape=jax.ShapeDtypeStruct((nb, ts, width), F32),
        compiler_params=_cparams(("parallel", "arbitrary")),
        name="paged_diff_attn",
    )(page_table.reshape(-1), q_t, *([cache_k] * PAGE_SPLITS), *([cache_v] * PAGE_SPLITS), k_new, v_new, g,
      lam4, subw3)


def _rope_tables(pos, dh):
    half = dh // 2
    inv = ROPE_THETA ** (-jnp.arange(half, dtype=F32) / half)
    ang = pos.astype(F32)[:, None] * inv[None]
    cos, sin = jnp.cos(ang), jnp.sin(ang)
    return jnp.concatenate([cos, cos], -1), jnp.concatenate([-sin, sin], -1)


def kernel(x_prompt, x_sample, state_conv, state_ssm, cache_k, cache_v, page_table, ssm_w_in, ssm_conv_w, ssm_conv_b, ssm_dt_bias, ssm_a_log, ssm_d, ssm_norm_w, ssm_w_out, attn_w_in, attn_lambda_q1, attn_lambda_k1, attn_lambda_q2, attn_lambda_k2, attn_subln_w, attn_w_out, ln_g, ln_b):
    bp, tp, d_model = x_prompt.shape
    bs, ts, _ = x_sample.shape
    depth = ln_g.shape[0]
    n_mixers = 2
    alpha = (2 * depth) ** 0.25
    past_len = page_table.shape[1] * cache_k.shape[2]

    _, _, nh, hp, n_state = state_ssm.shape
    d_inner = nh * hp
    conv_dim = state_conv.shape[-1]
    zx_w = d_inner + conv_dim
    n_pool, page, kheads, dh = cache_k.shape[1:]
    heads = kheads // 2
    dv = 2 * dh
    qk_dim = kheads * dh
    width = heads * dv
    assert n_state == LANES and dh == LANES and 2 * hp == LANES and qk_dim == width

    ln_g3, ln_b3 = ln_g.reshape(depth, 1, d_model), ln_b.reshape(depth, 1, d_model)
    ssm_prm = (ssm_conv_w, ssm_conv_b.reshape(-1, 1, conv_dim), ssm_dt_bias.reshape(-1, 1, nh),
               ssm_a_log.reshape(-1, 1, nh), ssm_d.reshape(-1, 1, nh), ssm_norm_w.reshape(-1, 1, d_inner))
    lam4 = jnp.stack([attn_lambda_q1, attn_lambda_k1, attn_lambda_q2, attn_lambda_k2], axis=1)
    subw3 = attn_subln_w.reshape(-1, 1, dv)
    state_ssm4 = state_ssm.reshape(state_ssm.shape[0], bs, d_inner, n_state)

    cos_p, sin_p = _rope_tables(jnp.tile(jnp.arange(tp), bp), dh)
    cos_s, sin_s = _rope_tables(jnp.tile(past_len + jnp.arange(ts), bs), dh)

    yp = x_prompt.reshape(bp * tp, d_model)
    ys = x_sample.reshape(bs * ts, d_model)
    ypb, ysb = yp.astype(BF16), ys.astype(BF16)
    conv_p, ssm_p, k_p, v_p, conv_s, ssm_s, k_s, v_s = ([] for _ in range(8))

    def ssm_layer(xb, j, nb, t, conv_buf, h0):
        zx = _matmul(xb, ssm_w_in, j, n=zx_w, name="ssm_in_proj").reshape(nb, t, zx_w)
        dt_raw = _matmul(xb, ssm_w_in, j, n_off=zx_w, n=nh, name="ssm_dt_proj").reshape(nb, t, nh)
        y, h = _ssd(zx, dt_raw, ssm_prm, j, out_dtype=BF16 if t >= SSM_CHUNK else F32, conv_buf=conv_buf, h0=h0)
        mix = _matmul(y.reshape(nb * t, d_inner), ssm_w_out, j, name="ssm_out_proj")
        assert t >= CONV_WIDTH - 1
        return mix, zx[:, t - (CONV_WIDTH - 1):, d_inner:], h.reshape(nb, nh, hp, n_state)

    def attn_project(xb, j, cos, sin, v_transposed):
        q = _matmul(xb, attn_w_in, j, n=qk_dim, out_dtype=BF16, rope=(cos, sin, dh ** -0.5 * LOG2E),
                    name="attn_q_proj")
        k = _matmul(xb, attn_w_in, j, n_off=qk_dim, n=qk_dim, rope=(cos, sin, 1.0), name="attn_k_proj")
        v = _matmul(xb, attn_w_in, j, n_off=2 * qk_dim, n=width, transposed_copy=v_transposed, name="attn_v_proj")
        g = _matmul(xb, attn_w_in, j, n_off=2 * qk_dim + width, n=width, name="attn_g_proj")
        return q, k, v, g

    for i in range(depth):
        j = i // n_mixers
        if i % n_mixers == 0:
            mp, cb, h = ssm_layer(ypb, j, bp, tp, None, None)
            conv_p.append(cb); ssm_p.append(h)
            ms, cb, h = ssm_layer(ysb, j, bs, ts, state_conv[j], state_ssm4[j])
            conv_s.append(cb); ssm_s.append(h)
        else:
            lam_init = 0.8 - 0.6 * math.exp(-0.3 * i)
            q, k, (v, vt), g = attn_project(ypb, j, cos_p, sin_p, True)
            og = _flash(q, k, vt, g, lam4, subw3, j, nb=bp, t=tp, heads=heads, dh=dh, lam_init=lam_init)
            mp = _matmul(og, attn_w_out, j, name="attn_out_proj")
            k_p.append(k.reshape(bp, tp, kheads, dh)); v_p.append(v.reshape(bp, tp, heads, dv))

            q, k, v, g = attn_project(ysb, j, cos_s, sin_s, False)
            q_t = q.reshape(bs, ts, kheads, dh).transpose(0, 2, 1, 3).reshape(bs, kheads * ts, dh)

            def tiled(a, n, d):
                a = a.reshape(bs, ts, n // MAP_TILE, MAP_TILE, d).transpose(0, 2, 1, 3, 4)
                a = a.reshape(bs, n // MAP_TILE, ts * MAP_TILE, d)
                return jnp.pad(a, ((0, 0), (0, 0), (0, -(ts * MAP_TILE) % LANES), (0, 0)))

            og = _paged(q_t, cache_k, cache_v, tiled(k, kheads, dh), tiled(v, heads, dv),
                        g.reshape(bs, ts, width), page_table, lam4, subw3, j, lam_init=lam_init)
            ms = _matmul(og.reshape(bs * ts, width), attn_w_out, j, name="attn_out_proj")
            k_s.append(k.reshape(bs, ts, kheads, dh)); v_s.append(v.reshape(bs, ts, heads, dv))
        yp, ypb = _add_ln(yp, mp, ln_g3, ln_b3, i, alpha)
        ys, ysb = _add_ln(ys, ms, ln_g3, ln_b3, i, alpha)

    return (yp.reshape(bp, tp, d_model), ys.reshape(bs, ts, d_model),
            jnp.stack(conv_p), jnp.stack(ssm_p), jnp.stack(k_p), jnp.stack(v_p),
            jnp.stack(conv_s), jnp.stack(ssm_s), jnp.stack(k_s), jnp.stack(v_s))
```
